```python
import math
import jax, jax.numpy as jnp
from jax import lax
import numpy as np

D_MODEL = 1024
BATCH = 1
SEQ = 16384
DEPTH = 2
DEC_BATCH = 32
DEC_SEQ = 1
PAST_LEN = 16384
PAGE_SIZE = 128

N_META = 16
ROPE_THETA = 500000.0
EPS = 1e-6
Q_BLOCK = 128
DA_HEADS = 8
DA_KV_HEADS = 4
DA_QK_DIM = 64
DA_V_DIM = 128
DSA_HEADS = 8
DSA_KV_HEADS = 4
DSA_HEAD_DIM = 128
IDX_HEADS = 8
IDX_DIM = 64
TOPK_MAX = 256
D_FF = 2816

DA_WIDTH = DA_HEADS * DA_V_DIM
DSA_WIDTH = DSA_HEADS * DSA_HEAD_DIM
SPLITS = (DA_HEADS * 2 * DA_QK_DIM, DA_KV_HEADS * 2 * DA_QK_DIM, DA_KV_HEADS * DA_V_DIM,
          DSA_HEADS * DSA_HEAD_DIM, DSA_KV_HEADS * DSA_HEAD_DIM, DSA_KV_HEADS * DSA_HEAD_DIM,
          IDX_HEADS * IDX_DIM, IDX_DIM, IDX_HEADS, D_MODEL, D_MODEL)
D_IN = sum(SPLITS)

kernel_name = 'meta_diffattn_dsa_macaron_hybrid_step'


def rmsnorm(x, g):
    x32 = x.astype(jnp.float32)
    y = x32 * lax.rsqrt(jnp.mean(x32 * x32, axis=-1, keepdims=True) + EPS)
    return (y * g.astype(jnp.float32)).astype(x.dtype)


def rope_partial(x, pos):
    d = x.shape[-1]
    r = d // 4
    half = r // 2
    inv = 1.0 / (ROPE_THETA ** (jnp.arange(half, dtype=jnp.float32) * (2.0 / r)))
    ang = pos.astype(jnp.float32)[:, None] * inv[None, :]
    shp = (pos.shape[0],) + (1,) * (x.ndim - 3) + (half,)
    cos = jnp.cos(ang).reshape(shp).astype(x.dtype)
    sin = jnp.sin(ang).reshape(shp).astype(x.dtype)
    x1, x2, rest = x[..., :half], x[..., half:r], x[..., r:]
    return jnp.concatenate([x1 * cos - x2 * sin, x2 * cos + x1 * sin, rest], axis=-1)


def swiglu(h, w_gu, w_down):
    g, u = jnp.split(h @ w_gu, 2, axis=-1)
    return (jax.nn.silu(g) * u) @ w_down


def project_heads(h, pos, w_in, qa_g, ka_g, qd_g, kd_g):
    B, S, _ = h.shape
    z = h @ w_in
    qa, ka, va, qd, kd, vd, qi, ki, wi, ga, gb = jnp.split(z, np.cumsum(SPLITS)[:-1].tolist(), axis=-1)
    qa = rope_partial(rmsnorm(qa.reshape(B, S, DA_HEADS, 2, DA_QK_DIM), qa_g), pos)
    ka = rope_partial(rmsnorm(ka.reshape(B, S, DA_KV_HEADS, 2, DA_QK_DIM), ka_g), pos)
    va = va.reshape(B, S, DA_KV_HEADS, DA_V_DIM)
    qd = rope_partial(rmsnorm(qd.reshape(B, S, DSA_HEADS, DSA_HEAD_DIM), qd_g), pos)
    kd = rope_partial(rmsnorm(kd.reshape(B, S, DSA_KV_HEADS, DSA_HEAD_DIM), kd_g), pos)
    vd = vd.reshape(B, S, DSA_KV_HEADS, DSA_HEAD_DIM)
    qi = rope_partial(qi.reshape(B, S, IDX_HEADS, IDX_DIM), pos)
    ki = rope_partial(ki, pos)
    wi = wi * (IDX_HEADS ** -0.5 * IDX_DIM ** -0.5)
    return (qa, ka, va, qd, kd, vd, qi, ki, wi), ga, gb


def diff_attn_core(q, k, v, q_pos, k_pos, lam):
    B, Q = q.shape[:2]
    G = DA_HEADS // DA_KV_HEADS
    qg = q.reshape(B, Q, DA_KV_HEADS, G, 2, DA_QK_DIM)
    s = jnp.einsum('bqkgmd,bskmd->bkgmqs', qg, k).astype(jnp.float32) * (DA_QK_DIM ** -0.5)
    mask = k_pos[None, :] <= q_pos[:, None]
    p = jax.nn.softmax(jnp.where(mask, s, -jnp.inf), axis=-1)
    a = p[:, :, :, 0] - lam * p[:, :, :, 1]
    o = jnp.einsum('bkgqs,bskd->bqkgd', a.astype(v.dtype), v)
    return o.reshape(B, Q, DA_HEADS, DA_V_DIM)


def dsa_select(qi, wi, ki, q_pos, k_pos, topk):
    sc = jax.nn.relu(jnp.einsum('bqhd,bsd->bqhs', qi, ki).astype(jnp.float32))
    score = jnp.einsum('bqhs,bqh->bqs', sc, wi.astype(jnp.float32))
    score = jnp.where(k_pos[None, None, :] <= q_pos[None, :, None], score, -jnp.inf)
    vals, idx = lax.top_k(score, topk)
    return idx, jnp.isfinite(vals)


def dsa_attend(q, k_sel, v_sel, valid):
    B, Q = q.shape[:2]
    G = DSA_HEADS // DSA_KV_HEADS
    qg = q.reshape(B, Q, DSA_KV_HEADS, G, DSA_HEAD_DIM)
    s = jnp.einsum('bqkgd,bqnkd->bqkgn', qg, k_sel).astype(jnp.float32) * (DSA_HEAD_DIM ** -0.5)
    p = jax.nn.softmax(jnp.where(valid[:, :, None, None, :], s, -jnp.inf), axis=-1)
    o = jnp.einsum('bqkgn,bqnkd->bqkgd', p.astype(v_sel.dtype), v_sel)
    return o.reshape(B, Q, DSA_WIDTH)


def gather_rows(rows, ix):
    return jax.vmap(lambda r, i: r[i])(rows, ix)


def prompt_mixers(heads, lam, topk):
    qa, ka, va, qd, kd, vd, qi, ki, wi = heads
    B, T = qa.shape[:2]
    n_blk = -(-T // Q_BLOCK)
    T_pad = n_blk * Q_BLOCK

    def to_blocks(a):
        a = jnp.pad(a, [(0, 0), (0, T_pad - T)] + [(0, 0)] * (a.ndim - 2))
        return jnp.moveaxis(a.reshape((B, n_blk, Q_BLOCK) + a.shape[2:]), 1, 0)

    k_pos = jnp.arange(T, dtype=jnp.int32)
    q_pos_blocks = jnp.arange(T_pad, dtype=jnp.int32).reshape(n_blk, Q_BLOCK)

    def one_block(args):
        qa_b, qd_b, qi_b, wi_b, qp = args
        oa = diff_attn_core(qa_b, ka, va, qp, k_pos, lam)
        idx, valid = dsa_select(qi_b, wi_b, ki, qp, k_pos, topk)
        od = dsa_attend(qd_b, gather_rows(kd, idx), gather_rows(vd, idx), valid)
        return oa, od

    oa, od = lax.map(one_block, (to_blocks(qa), to_blocks(qd), to_blocks(qi), to_blocks(wi), q_pos_blocks))

    def unblock(o):
        return jnp.moveaxis(o, 0, 1).reshape((B, T_pad) + o.shape[3:])[:, :T]

    return unblock(oa), unblock(od)


def sample_mixers(heads, lam, topk, pool_da_k, pool_da_v, pool_dsa_k, pool_dsa_v, pool_idx_k, page_table):
    qa, ka, va, qd, kd, vd, qi, ki, wi = heads
    DB, Sq = qa.shape[:2]
    past = page_table.shape[1] * PAGE_SIZE

    def paged(pool):
        return pool[page_table].reshape((DB, past) + pool.shape[2:])

    q_pos = past + jnp.arange(Sq, dtype=jnp.int32)
    k_pos = jnp.arange(past + Sq, dtype=jnp.int32)
    ka_all = jnp.concatenate([paged(pool_da_k), ka], axis=1)
    va_all = jnp.concatenate([paged(pool_da_v), va], axis=1)
    oa = diff_attn_core(qa, ka_all, va_all, q_pos, k_pos, lam)
    ki_all = jnp.concatenate([paged(pool_idx_k), ki], axis=1)
    idx, valid = dsa_select(qi, wi, ki_all, q_pos, k_pos, topk)
    past_ix = jnp.minimum(idx, past - 1)
    page_ix = jnp.take_along_axis(page_table, (past_ix // PAGE_SIZE).reshape(DB, -1), axis=1).reshape(idx.shape)
    phys = page_ix * PAGE_SIZE + past_ix % PAGE_SIZE
    new_ix = jnp.clip(idx - past, 0, Sq - 1)
    is_past = (idx < past)[..., None, None]
    flat_k = pool_dsa_k.reshape((-1,) + pool_dsa_k.shape[2:])
    flat_v = pool_dsa_v.reshape((-1,) + pool_dsa_v.shape[2:])
    k_sel = jnp.where(is_past, flat_k[phys], gather_rows(kd, new_ix))
    v_sel = jnp.where(is_past, flat_v[phys], gather_rows(vd, new_ix))
    od = dsa_attend(qd, k_sel, v_sel, valid)
    return oa, od


def trunk_layer(x, pos, l, mixer, params):
    (norm_ffn1, ffn1_w_gu, ffn1_w_down, norm_mix, w_in, da_q_norm, da_k_norm, dsa_q_norm, dsa_k_norm,
     da_lambda, da_out_norm, w_branch_a, w_branch_b, w_out, norm_ffn2, ffn2_w_gu, ffn2_w_down) = params
    x = x + 0.5 * swiglu(rmsnorm(x, norm_ffn1[l]), ffn1_w_gu[l], ffn1_w_down[l])
    h = rmsnorm(x, norm_mix[l])
    heads, ga, gb = project_heads(h, pos, w_in[l], da_q_norm[l], da_k_norm[l], dsa_q_norm[l], dsa_k_norm[l])
    lam_init = 0.8 - 0.6 * math.exp(-0.3 * l)
    lp = da_lambda[l].astype(jnp.float32)
    lam = jnp.exp(jnp.sum(lp[0] * lp[1])) - jnp.exp(jnp.sum(lp[2] * lp[3])) + lam_init
    oa, od = mixer(heads, lam, l)
    oa = (rmsnorm(oa, da_out_norm[l]) * (1.0 - lam_init)).reshape(oa.shape[:2] + (DA_WIDTH,))
    merged = jax.nn.sigmoid(ga) * (oa @ w_branch_a[l]) + jax.nn.sigmoid(gb) * (od @ w_branch_b[l])
    x = x + merged @ w_out[l]
    x = x + 0.5 * swiglu(rmsnorm(x, norm_ffn2[l]), ffn2_w_gu[l], ffn2_w_down[l])
    qa, ka, va, qd, kd, vd, qi, ki, wi = heads
    return x, (ka, va, kd, vd, ki)


def setup_inputs(seed: int = 0) -> dict:
    key = jax.random.key(seed)
    ks = iter(jax.random.split(key, 40))
    f32 = jnp.float32

    def nrm(shape, scale=1.0):
        return jax.random.normal(next(ks), shape, f32) * scale

    def gain(shape):
        return 1.0 + 0.02 * nrm(shape)

    n_pages = PAST_LEN // PAGE_SIZE
    used = DEC_BATCH * n_pages
    n_pool = used + max(1, used // 4)
    page_table = jax.random.permutation(next(ks), n_pool)[:used].reshape(DEC_BATCH, n_pages).astype(jnp.int32)
    return {
        'x_prompt': nrm((BATCH, SEQ, D_MODEL)),
        'x_sample': nrm((DEC_BATCH, DEC_SEQ, D_MODEL)),
        'cache_da_k': nrm((DEPTH, n_pool, PAGE_SIZE, DA_KV_HEADS, 2, DA_QK_DIM)),
        'cache_da_v': nrm((DEPTH, n_pool, PAGE_SIZE, DA_KV_HEADS, DA_V_DIM)),
        'cache_dsa_k': nrm((DEPTH, n_pool, PAGE_SIZE, DSA_KV_HEADS, DSA_HEAD_DIM)),
        'cache_dsa_v': nrm((DEPTH, n_pool, PAGE_SIZE, DSA_KV_HEADS, DSA_HEAD_DIM)),
        'cache_idx_k': nrm((DEPTH, n_pool, PAGE_SIZE, IDX_DIM)),
        'page_table': page_table,
        'meta_tokens': nrm((N_META, D_MODEL)),
        'norm_ffn1': gain((DEPTH, D_MODEL)),
        'ffn1_w_gu': nrm((DEPTH, D_MODEL, 2 * D_FF), D_MODEL ** -0.5),
        'ffn1_w_down': nrm((DEPTH, D_FF, D_MODEL), D_FF ** -0.5),
        'norm_mix': gain((DEPTH, D_MODEL)),
        'w_in': nrm((DEPTH, D_MODEL, D_IN), D_MODEL ** -0.5),
        'da_q_norm': gain((DEPTH, DA_QK_DIM)),
        'da_k_norm': gain((DEPTH, DA_QK_DIM)),
        'dsa_q_norm': gain((DEPTH, DSA_HEAD_DIM)),
        'dsa_k_norm': gain((DEPTH, DSA_HEAD_DIM)),
        'da_lambda': nrm((DEPTH, 4, DA_QK_DIM), 0.1),
        'da_out_norm': gain((DEPTH, DA_V_DIM)),
        'w_branch_a': nrm((DEPTH, DA_WIDTH, D_MODEL), DA_WIDTH ** -0.5),
        'w_branch_b': nrm((DEPTH, DSA_WIDTH, D_MODEL), DSA_WIDTH ** -0.5),
        'w_out': nrm((DEPTH, D_MODEL, D_MODEL), D_MODEL ** -0.5),
        'norm_ffn2': gain((DEPTH, D_MODEL)),
        'ffn2_w_gu': nrm((DEPTH, D_MODEL, 2 * D_FF), D_MODEL ** -0.5),
        'ffn2_w_down': nrm((DEPTH, D_FF, D_MODEL), D_FF ** -0.5),
    }


def reference(x_prompt, x_sample, cache_da_k, cache_da_v, cache_dsa_k, cache_dsa_v, cache_idx_k, page_table,
              meta_tokens, norm_ffn1, ffn1_w_gu, ffn1_w_down, norm_mix, w_in, da_q_norm, da_k_norm,
              dsa_q_norm, dsa_k_norm, da_lambda, da_out_norm, w_branch_a, w_branch_b, w_out,
              norm_ffn2, ffn2_w_gu, ffn2_w_down):
    params = (norm_ffn1, ffn1_w_gu, ffn1_w_down, norm_mix, w_in, da_q_norm, da_k_norm, dsa_q_norm, dsa_k_norm,
              da_lambda, da_out_norm, w_branch_a, w_branch_b, w_out, norm_ffn2, ffn2_w_gu, ffn2_w_down)
    B = x_prompt.shape[0]
    meta = jnp.broadcast_to(meta_tokens.astype(x_prompt.dtype)[None], (B, N_META, D_MODEL))
    xp = jnp.concatenate([meta, x_prompt], axis=1)
    T = xp.shape[1]
    pos_p = jnp.arange(T, dtype=jnp.int32)
    topk_p = min(TOPK_MAX, T // 4)
    rows_p = []
    for l in range(DEPTH):
        mixer_p = lambda heads, lam, l: prompt_mixers(heads, lam, topk_p)
        xp, rows = trunk_layer(xp, pos_p, l, mixer_p, params)
        rows_p.append(rows)
    y_prompt = xp[:, N_META:]
    past = page_table.shape[1] * PAGE_SIZE
    Sq = x_sample.shape[1]
    pos_s = past + jnp.arange(Sq, dtype=jnp.int32)
    topk_s = min(TOPK_MAX, (past + Sq) // 4)
    xs = x_sample
    rows_s = []
    for l in range(DEPTH):
        mixer_s = lambda heads, lam, l: sample_mixers(heads, lam, topk_s, cache_da_k[l], cache_da_v[l],
                                                      cache_dsa_k[l], cache_dsa_v[l], cache_idx_k[l], page_table)
        xs, rows = trunk_layer(xs, pos_s, l, mixer_s, params)
        rows_s.append(rows)
    y_sample = xs

    def stack(rows, i):
        return jnp.stack([r[i] for r in rows], axis=0)

    return (y_prompt, y_sample,
            stack(rows_p, 0), stack(rows_p, 1), stack(rows_p, 2), stack(rows_p, 3), stack(rows_p, 4),
            stack(rows_s, 0), stack(rows_s, 1), stack(rows_s, 2), stack(rows_s, 3), stack(rows_s, 4))
```

```python
import functools
import math

import jax
import jax.numpy as jnp
import numpy as np
from jax import lax
from jax.experimental import pallas as pl
from jax.experimental.pallas import tpu as pltpu

F32 = jnp.float32
BF16 = jnp.bfloat16
I32 = jnp.int32

N_META = 16
ROPE_THETA = 500000.0
EPS = 1e-6
PAGE_SIZE = 128
DA_HEADS, DA_KV_HEADS, DA_QK_DIM, DA_V_DIM = 8, 4, 64, 128
DSA_HEADS, DSA_KV_HEADS, DSA_HEAD_DIM = 8, 4, 128
IDX_HEADS, IDX_DIM = 8, 64
TOPK_MAX = 256
SPLITS = (DA_HEADS * 2 * DA_QK_DIM, DA_KV_HEADS * 2 * DA_QK_DIM, DA_KV_HEADS * DA_V_DIM,
          DSA_HEADS * DSA_HEAD_DIM, DSA_KV_HEADS * DSA_HEAD_DIM, DSA_KV_HEADS * DSA_HEAD_DIM,
          IDX_HEADS * IDX_DIM, IDX_DIM, IDX_HEADS)

LANES = 128
MXU_COLS = 256
VMEM_LIMIT_BYTES = 56 * 1024 * 1024

PROJ_TN = MXU_COLS
_OFF = {}
_o = 0
for _name, _w in (("qa", 1024), ("ka", 512), ("va", 512), ("qd", 1024), ("kd", 512), ("vd", 512),
                  ("ga", 1024), ("gb", 1024), ("qi", 512), ("kw", PROJ_TN)):
    _OFF[_name] = (_o, _w)
    _o += _w
PROJ_N = _o

NEG = -1e30
INT_MIN = -2 ** 31

TQ = 256
TK = 512
TQ_SEL = 128
T_ALIGN = 1536
TM_FFN = 768
TM_PROJ = 512
TM_MERGE = 512
FF_CHUNK = 256
PAGES_PER_STEP = 8


def _cparams(*sem):
    return pltpu.CompilerParams(dimension_semantics=sem, vmem_limit_bytes=VMEM_LIMIT_BYTES)


def _ffn_kernel(x_ref, g_ref, wg_ref, wu_ref, wd_ref, o_ref, h_scr):
    c = pl.program_id(1)

    @pl.when(c == 0)
    def _():
        x = x_ref[...]
        ms = jnp.mean(x * x, axis=-1, keepdims=True)
        h_scr[...] = ((x * lax.rsqrt(ms + EPS)) * g_ref[...]).astype(BF16)
        o_ref[...] = x

    h = h_scr[...]
    g = jnp.dot(h, wg_ref[...], preferred_element_type=F32)
    u = jnp.dot(h, wu_ref[...], preferred_element_type=F32)
    a = ((g * jax.nn.sigmoid(g)) * u).astype(BF16)
    o_ref[...] += 0.5 * jnp.dot(a, wd_ref[...], preferred_element_type=F32)


def _ffn(x, gain, w_gu, w_down, tm):
    m, d = x.shape
    d_ff = w_down.shape[0]
    n_c = d_ff // FF_CHUNK
    return pl.pallas_call(
        _ffn_kernel,
        out_shape=jax.ShapeDtypeStruct((m, d), F32),
        grid=(m // tm, n_c),
        in_specs=[
            pl.BlockSpec((tm, d), lambda i, c: (i, 0)),
            pl.BlockSpec((1, d), lambda i, c: (0, 0)),
            pl.BlockSpec((d, FF_CHUNK), lambda i, c: (0, c)),
            pl.BlockSpec((d, FF_CHUNK), lambda i, c: (0, c + n_c)),
            pl.BlockSpec((FF_CHUNK, d), lambda i, c: (c, 0)),
        ],
        out_specs=pl.BlockSpec((tm, d), lambda i, c: (i, 0)),
        scratch_shapes=[pltpu.VMEM((tm, d), BF16)],
        compiler_params=_cparams("parallel", "arbitrary"),
        name="ffn",
    )(x, gain.reshape(1, d), w_gu, w_gu, w_down)


def _seg_rms(x, gain, seg):
    cols = []
    for b in range(x.shape[1] // LANES):
        blk = x[:, b * LANES:(b + 1) * LANES]
        sq = blk * blk
        if seg == LANES:
            ms = jnp.sum(sq, axis=-1, keepdims=True) * (1.0 / seg)
        else:
            lane = lax.broadcasted_iota(I32, blk.shape, 1)
            lo = jnp.sum(jnp.where(lane < seg, sq, 0.0), axis=-1, keepdims=True)
            hi = jnp.sum(jnp.where(lane >= seg, sq, 0.0), axis=-1, keepdims=True)
            ms = jnp.where(lane < seg, lo, hi) * (1.0 / seg)
        cols.append((blk * lax.rsqrt(ms + EPS)) * gain)
    return jnp.concatenate(cols, axis=1)


def _rope(x, cos, sin, half):
    n = x.shape[1]
    reps = n // LANES
    c = jnp.concatenate([cos] * reps, axis=1)
    s = jnp.concatenate([sin] * reps, axis=1)
    up = pltpu.roll(x, n - half, 1)
    dn = pltpu.roll(x, half, 1)
    lane = lax.broadcasted_iota(I32, x.shape, 1)
    partner = jnp.where((lane & (2 * half - 1)) < half, up, dn)
    return x * c + partner * s


def _proj_kernel(x_ref, g_ref, w_ref, c64_ref, s64_ref, c128_ref, s128_ref,
                 qa_g, ka_g, qd_g, kd_g, o_ref, h_scr):
    j = pl.program_id(1)

    @pl.when(j == 0)
    def _():
        x = x_ref[...]
        ms = jnp.mean(x * x, axis=-1, keepdims=True)
        h_scr[...] = ((x * lax.rsqrt(ms + EPS)) * g_ref[...]).astype(BF16)

    z = jnp.dot(h_scr[...], w_ref[...], preferred_element_type=F32)

    def tile_of(name):
        off, width = _OFF[name]
        return off // PROJ_TN, (off + width) // PROJ_TN

    def in_group(name):
        lo, hi = tile_of(name)
        return jnp.logical_and(j >= lo, j < hi)

    h64, h128 = DA_QK_DIM // 8, DSA_HEAD_DIM // 8

    @pl.when(in_group("qa"))
    def _():
        o_ref[...] = _rope(_seg_rms(z, qa_g[...], DA_QK_DIM), c64_ref[...], s64_ref[...], h64)

    @pl.when(in_group("ka"))
    def _():
        o_ref[...] = _rope(_seg_rms(z, ka_g[...], DA_QK_DIM), c64_ref[...], s64_ref[...], h64)

    @pl.when(in_group("qd"))
    def _():
        o_ref[...] = _rope(_seg_rms(z, qd_g[...], DSA_HEAD_DIM), c128_ref[...], s128_ref[...], h128)

    @pl.when(in_group("kd"))
    def _():
        o_ref[...] = _rope(_seg_rms(z, kd_g[...], DSA_HEAD_DIM), c128_ref[...], s128_ref[...], h128)

    @pl.when(jnp.logical_or(in_group("va"), in_group("vd")))
    def _():
        o_ref[...] = z

    @pl.when(in_group("qi"))
    def _():
        o_ref[...] = _rope(z, c64_ref[...], s64_ref[...], h64)

    @pl.when(jnp.logical_or(in_group("ga"), in_group("gb")))
    def _():
        o_ref[...] = jax.nn.sigmoid(z)

    @pl.when(in_group("kw"))
    def _():
        r = _rope(z, c64_ref[...], s64_ref[...], h64)
        lane = lax.broadcasted_iota(I32, z.shape, 1)
        wi_scale = IDX_HEADS ** -0.5 * IDX_DIM ** -0.5
        o_ref[...] = jnp.where(lane < IDX_DIM, r, z * wi_scale)


def _proj(x, gain, w_packed, tabs, gains, tm):
    m, d = x.shape
    n = w_packed.shape[1]
    c64, s64, c128, s128 = tabs
    row = lambda i, j: (i, 0)
    const = lambda i, j: (0, 0)
    return pl.pallas_call(
        _proj_kernel,
        out_shape=jax.ShapeDtypeStruct((m, n), F32),
        grid=(m // tm, n // PROJ_TN),
        in_specs=[
            pl.BlockSpec((tm, d), row),
            pl.BlockSpec((1, d), const),
            pl.BlockSpec((d, PROJ_TN), lambda i, j: (0, j)),
            pl.BlockSpec((tm, LANES), row), pl.BlockSpec((tm, LANES), row),
            pl.BlockSpec((tm, LANES), row), pl.BlockSpec((tm, LANES), row),
            pl.BlockSpec((1, LANES), const), pl.BlockSpec((1, LANES), const),
            pl.BlockSpec((1, LANES), const), pl.BlockSpec((1, LANES), const),
        ],
        out_specs=pl.BlockSpec((tm, PROJ_TN), lambda i, j: (i, j)),
        scratch_shapes=[pltpu.VMEM((tm, d), BF16)],
        compiler_params=_cparams("parallel", "arbitrary"),
        name="proj",
    )(x, gain.reshape(1, d), w_packed, c64, s64, c128, s128, *gains)


def _merge_kernel(lam_init, x_ref, oa_ref, od_ref, ga_ref, gb_ref, g_ref, wa_ref, wb_ref, wo_ref,
                  o_ref, a_scr):
    for h in range(DA_HEADS):
        blk = oa_ref[:, h * DA_V_DIM:(h + 1) * DA_V_DIM]
        ms = jnp.mean(blk * blk, axis=-1, keepdims=True)
        y = ((blk * lax.rsqrt(ms + EPS)) * g_ref[...]) * (1.0 - lam_init)
        a_scr[:, h * DA_V_DIM:(h + 1) * DA_V_DIM] = y.astype(BF16)
    a = jnp.dot(a_scr[...], wa_ref[...], preferred_element_type=F32)
    b = jnp.dot(od_ref[...].astype(BF16), wb_ref[...], preferred_element_type=F32)
    merged = ga_ref[...] * a + gb_ref[...] * b
    o_ref[...] = x_ref[...] + jnp.dot(merged.astype(BF16), wo_ref[...], preferred_element_type=F32)


def _merge(x, oa, od, z, gain, wa, wb, wo, lam_init, tm):
    m, d = x.shape
    row = lambda i: (i, 0)
    const = lambda i: (0, 0)
    ga_blk = _OFF["ga"][0] // d
    gb_blk = _OFF["gb"][0] // d
    return pl.pallas_call(
        functools.partial(_merge_kernel, lam_init),
        out_shape=jax.ShapeDtypeStruct((m, d), F32),
        grid=(m // tm,),
        in_specs=[
            pl.BlockSpec((tm, d), row), pl.BlockSpec((tm, d), row), pl.BlockSpec((tm, d), row),
            pl.BlockSpec((tm, d), lambda i: (i, ga_blk)), pl.BlockSpec((tm, d), lambda i: (i, gb_blk)),
            pl.BlockSpec((1, DA_V_DIM), const),
            pl.BlockSpec((d, d), const), pl.BlockSpec((d, d), const), pl.BlockSpec((d, d), const),
        ],
        out_specs=pl.BlockSpec((tm, d), row),
        scratch_shapes=[pltpu.VMEM((tm, d), BF16)],
        compiler_params=_cparams("parallel"),
        name="merge",
    )(x, oa, od, z, z, gain.reshape(1, DA_V_DIM), wa, wb, wo)


def _lambda_of(lp, lam_init):
    a = jnp.sum(lp[0:1, :] * lp[1:2, :], axis=-1, keepdims=True)
    b = jnp.sum(lp[2:3, :] * lp[3:4, :], axis=-1, keepdims=True)
    return jnp.exp(a) - jnp.exp(b) + lam_init


def _softmax_step(s, v, m_scr, l_scr, acc_scr):
    m_prev = m_scr[...]
    m_new = jnp.maximum(m_prev, jnp.max(s, axis=-1, keepdims=True))
    alpha = jnp.exp(m_prev - m_new)
    p = jnp.exp(s - m_new)
    l_scr[...] = alpha * l_scr[...] + jnp.sum(p, axis=-1, keepdims=True)
    acc_scr[...] = alpha * acc_scr[...] + jnp.dot(p.astype(BF16), v, preferred_element_type=F32)
    m_scr[...] = m_new


def _diff_attn_kernel(lam_init, q_ref, kt_ref, v_ref, lp_ref, o_ref, lhs_scr, m_scr, l_scr, acc_scr):
    qt = pl.program_id(1)
    q = q_ref[...]
    lane = lax.broadcasted_iota(I32, (TQ, LANES), 1)
    for g in range(2):
        qg = q[:, g * LANES:(g + 1) * LANES]
        for mp in range(2):
            keep = (lane < DA_QK_DIM) if mp == 0 else (lane >= DA_QK_DIM)
            r = (g * 2 + mp) * TQ
            lhs_scr[r:r + TQ, :] = jnp.where(keep, qg, jnp.zeros_like(qg))
    m_scr[...] = jnp.full(m_scr.shape, NEG, F32)
    l_scr[...] = jnp.zeros(l_scr.shape, F32)
    acc_scr[...] = jnp.zeros(acc_scr.shape, F32)
    n_k = ((qt + 1) * TQ + TK - 1) // TK
    rows = 4 * TQ
    qpos = qt * TQ + (lax.broadcasted_iota(I32, (rows, TK), 0) & (TQ - 1))
    col = lax.broadcasted_iota(I32, (rows, TK), 1)

    def body(kt, carry):
        s = jnp.dot(lhs_scr[...], kt_ref[0, kt], preferred_element_type=F32)
        s = jnp.where(col + kt * TK <= qpos, s, NEG)
        v = v_ref[pl.ds(pl.multiple_of(kt * TK, TK), TK), :]
        _softmax_step(s, v, m_scr, l_scr, acc_scr)
        return carry

    lax.fori_loop(0, n_k, body, 0)
    o = acc_scr[...] / l_scr[...]
    lam = _lambda_of(lp_ref[...], lam_init)
    for g in range(2):
        r = g * 2 * TQ
        o_ref[:, g * DA_V_DIM:(g + 1) * DA_V_DIM] = o[r:r + TQ, :] - lam * o[r + TQ:r + 2 * TQ, :]


def _diff_attn(q, kt, v, lp, lam_init):
    t = q.shape[0]
    n_kt = t // TK
    return pl.pallas_call(
        functools.partial(_diff_attn_kernel, lam_init),
        out_shape=jax.ShapeDtypeStruct((t, DA_HEADS * DA_V_DIM), F32),
        grid=(DA_KV_HEADS, t // TQ),
        in_specs=[
            pl.BlockSpec((TQ, 2 * LANES), lambda kv, i: (i, kv)),
            pl.BlockSpec((1, n_kt, LANES, TK), lambda kv, i: (kv, 0, 0, 0)),
            pl.BlockSpec((t, DA_V_DIM), lambda kv, i: (0, kv)),
            pl.BlockSpec((4, DA_QK_DIM), lambda kv, i: (0, 0)),
        ],
        out_specs=pl.BlockSpec((TQ, 2 * DA_V_DIM), lambda kv, i: (i, kv)),
        scratch_shapes=[pltpu.VMEM((4 * TQ, LANES), BF16), pltpu.VMEM((4 * TQ, 1), F32),
                        pltpu.VMEM((4 * TQ, 1), F32), pltpu.VMEM((4 * TQ, DA_V_DIM), F32)],
        compiler_params=_cparams("parallel", "arbitrary"),
        name="diff_attn",
    )(q, kt, v, lp)


def _float_key(s):
    s = jnp.where(s == 0.0, 0.0, s)
    bits = pltpu.bitcast(s, I32)
    return jnp.where(bits < 0, bits ^ jnp.int32(0x7FFFFFFF), bits)


def _kth_largest_key(count_ge, rows, k):
    def bit_body(i, t):
        cand = t + lax.shift_left(jnp.int32(1), 31 - i)
        return jnp.where(count_ge(cand) >= k, cand, t)

    return lax.fori_loop(0, 32, bit_body, jnp.full((rows, 1), INT_MIN, I32))


def _select_kernel(topk, qi_ref, wi_ref, ki_ref, o_ref, lhs_scr, key_scr):
    qt = pl.program_id(0)
    n_kt = key_scr.shape[0]
    n_k = ((qt + 1) * TQ_SEL + TK - 1) // TK
    lane = lax.broadcasted_iota(I32, (TQ_SEL, LANES), 1)
    for h in range(IDX_HEADS):
        pair = qi_ref[:, (h // 2) * LANES:(h // 2 + 1) * LANES]
        keep = (lane < IDX_DIM) if h % 2 == 0 else (lane >= IDX_DIM)
        lhs_scr[h * TQ_SEL:(h + 1) * TQ_SEL, :] = jnp.where(keep, pair, jnp.zeros_like(pair))
    wi = wi_ref[...]
    w_cols = [wi[:, h:h + 1] for h in range(IDX_HEADS)]
    qpos = qt * TQ_SEL + lax.broadcasted_iota(I32, (TQ_SEL, TK), 0)
    col = lax.broadcasted_iota(I32, (TQ_SEL, TK), 1)

    def score_body(kt, carry):
        d = jnp.dot(lhs_scr[...], ki_ref[kt], preferred_element_type=F32)
        sc = jnp.zeros((TQ_SEL, TK), F32)
        for h in range(IDX_HEADS):
            sc = sc + jnp.maximum(d[h * TQ_SEL:(h + 1) * TQ_SEL, :], 0.0) * w_cols[h]
        key_scr[kt] = jnp.where(col + kt * TK <= qpos, _float_key(sc), INT_MIN)
        return carry

    lax.fori_loop(0, n_k, score_body, 0)

    def count(pred):
        def body(kt, acc):
            hit = jnp.where(pred(key_scr[kt], col + kt * TK), 1.0, 0.0)
            for c in range(TK // LANES):
                acc = acc + hit[:, c * LANES:(c + 1) * LANES]
            return acc

        acc = lax.fori_loop(0, n_k, body, jnp.zeros((TQ_SEL, LANES), F32))
        return jnp.sum(acc, axis=-1, keepdims=True)

    kf = float(topk)
    t = _kth_largest_key(lambda cand: count(lambda keys, idx: keys >= cand), TQ_SEL, kf)
    t = jnp.maximum(t, INT_MIN + 1)
    n_ge = count(lambda keys, idx: keys >= t)

    @pl.when(jnp.max(n_ge) > kf)
    def _():
        need = kf - count(lambda keys, idx: keys > t)

        def bit_body(i, x):
            cand = x + lax.shift_left(jnp.int32(1), 14 - i)
            below = count(lambda keys, idx: jnp.logical_and(keys == t, idx < cand))
            return jnp.where(below < need, cand, x)

        last = lax.fori_loop(0, 15, bit_body, jnp.zeros((TQ_SEL, 1), I32))

        def drop_body(kt, carry):
            keys = key_scr[kt]
            drop = jnp.logical_and(keys == t, col + kt * TK > last)
            key_scr[kt] = jnp.where(drop, INT_MIN, keys)
            return carry

        lax.fori_loop(0, n_k, drop_body, 0)

    def write_body(kt, carry):
        o_ref[kt] = jnp.where(key_scr[kt] >= t, 0.0, NEG).astype(BF16)
        return carry

    lax.fori_loop(0, n_k, write_body, 0)

    def fill_body(kt, carry):
        o_ref[kt] = jnp.full((TQ_SEL, TK), NEG, BF16)
        return carry

    lax.fori_loop(n_k, n_kt, fill_body, 0)


def _select(qi, wi, ki2, topk):
    t = qi.shape[0]
    n_kt = t // TK
    assert t < 2 ** 15
    return pl.pallas_call(
        functools.partial(_select_kernel, topk),
        out_shape=jax.ShapeDtypeStruct((n_kt, t, TK), BF16),
        grid=(t // TQ_SEL,),
        in_specs=[
            pl.BlockSpec((TQ_SEL, IDX_HEADS * IDX_DIM), lambda i: (i, 0)),
            pl.BlockSpec((TQ_SEL, LANES), lambda i: (i, 0)),
            pl.BlockSpec((n_kt, LANES, TK), lambda i: (0, 0, 0)),
        ],
        out_specs=pl.BlockSpec((n_kt, TQ_SEL, TK), lambda i: (0, i, 0)),
        scratch_shapes=[pltpu.VMEM((IDX_HEADS * TQ_SEL, LANES), BF16),
                        pltpu.VMEM((n_kt, TQ_SEL, TK), I32)],
        compiler_params=_cparams("parallel"),
        name="dsa_select",
    )(qi, wi, ki2)


def _dsa_attn_kernel(q_ref, kt_ref, v_ref, b_ref, o_ref, lhs_scr, m_scr, l_scr, acc_scr):
    qt = pl.program_id(1)
    for g in range(2):
        lhs_scr[g * TQ:(g + 1) * TQ, :] = q_ref[:, g * LANES:(g + 1) * LANES]
    m_scr[...] = jnp.full(m_scr.shape, NEG, F32)
    l_scr[...] = jnp.zeros(l_scr.shape, F32)
    acc_scr[...] = jnp.zeros(acc_scr.shape, F32)
    n_k = ((qt + 1) * TQ + TK - 1) // TK

    def body(kt, carry):
        s = jnp.dot(lhs_scr[...], kt_ref[0, kt], preferred_element_type=F32)
        bias = b_ref[kt].astype(F32)
        s = s + jnp.concatenate([bias, bias], axis=0)
        v = v_ref[pl.ds(pl.multiple_of(kt * TK, TK), TK), :]
        _softmax_step(s, v, m_scr, l_scr, acc_scr)
        return carry

    lax.fori_loop(0, n_k, body, 0)
    o = acc_scr[...] / l_scr[...]
    for g in range(2):
        o_ref[:, g * DSA_HEAD_DIM:(g + 1) * DSA_HEAD_DIM] = o[g * TQ:(g + 1) * TQ, :]


def _dsa_attn(q, kt, v, bias):
    t = q.shape[0]
    n_kt = t // TK
    return pl.pallas_call(
        _dsa_attn_kernel,
        out_shape=jax.ShapeDtypeStruct((t, DSA_HEADS * DSA_HEAD_DIM), F32),
        grid=(DSA_KV_HEADS, t // TQ),
        in_specs=[
            pl.BlockSpec((TQ, 2 * LANES), lambda kv, i: (i, kv)),
            pl.BlockSpec((1, n_kt, LANES, TK), lambda kv, i: (kv, 0, 0, 0)),
            pl.BlockSpec((t, DSA_HEAD_DIM), lambda kv, i: (0, kv)),
            pl.BlockSpec((n_kt, TQ, TK), lambda kv, i: (0, i, 0)),
        ],
        out_specs=pl.BlockSpec((TQ, 2 * DSA_HEAD_DIM), lambda kv, i: (i, kv)),
        scratch_shapes=[pltpu.VMEM((2 * TQ, LANES), BF16), pltpu.VMEM((2 * TQ, 1), F32),
                        pltpu.VMEM((2 * TQ, 1), F32), pltpu.VMEM((2 * TQ, DSA_HEAD_DIM), F32)],
        compiler_params=_cparams("parallel", "arbitrary"),
        name="dsa_attn",
    )(q, kt, v, bias)


STREAM_ROWS = 16


def _paged_attn_kernel(mode, lam_init, n_pages, *refs):
    npg = PAGES_PER_STEP
    pt_ref, wq_ref = refs[0], refs[1]
    k_refs = refs[2:2 + npg]
    v_refs = refs[2 + npg:2 + 2 * npg]
    knew_ref, vnew_ref = refs[2 + 2 * npg], refs[3 + 2 * npg]
    pos = 4 + 2 * npg
    if mode == "da":
        lp_ref = refs[pos]
        pos += 1
    else:
        bias_ref, bnew_ref = refs[pos], refs[pos + 1]
        pos += 2
    o_ref, m_scr, l_scr, acc_scr = refs[pos:pos + 4]
    j = pl.program_id(1)
    wq = wq_ref[0]

    @pl.when(j == 0)
    def _():
        s_new = jnp.sum(wq.astype(F32) * knew_ref[0], axis=-1, keepdims=True)
        if mode == "dsa":
            s_new = s_new + bnew_ref[0][:, 0:1].astype(F32)
        m_scr[...] = s_new
        l_scr[...] = jnp.ones(l_scr.shape, F32)
        acc_scr[...] = jnp.broadcast_to(vnew_ref[0], acc_scr.shape)

    k = jnp.concatenate([r[0] for r in k_refs], axis=0).astype(BF16)
    v = jnp.concatenate([r[0] for r in v_refs], axis=0).astype(BF16)
    s = lax.dot_general(wq, k, (((1,), (1,)), ((), ())), preferred_element_type=F32)
    if mode == "dsa":
        s = s + bias_ref[0].astype(F32)
    _softmax_step(s, v, m_scr, l_scr, acc_scr)

    @pl.when(j == n_pages // npg - 1)
    def _():
        o = acc_scr[...] / l_scr[...]
        if mode == "da":
            lam = _lambda_of(lp_ref[...], lam_init)
            for kv in range(DA_KV_HEADS):
                for g in range(2):
                    r = kv * 4 + g * 2
                    blk = slice(kv * DA_V_DIM, (kv + 1) * DA_V_DIM)
                    h = kv * 2 + g
                    o_ref[0, :, h * DA_V_DIM:(h + 1) * DA_V_DIM] = o[r:r + 1, blk] - lam * o[r + 1:r + 2, blk]
        else:
            for kv in range(DSA_KV_HEADS):
                for g in range(2):
                    r = kv * 2 + g
                    o_ref[0, :, r * DSA_HEAD_DIM:(r + 1) * DSA_HEAD_DIM] = (
                        o[r:r + 1, kv * DSA_HEAD_DIM:(kv + 1) * DSA_HEAD_DIM])


def _paged_attn(mode, page_table, wq, k_pool, v_pool, k_new, v_new, extra, lam_init):
    nb, n_pages = page_table.shape
    npg = PAGES_PER_STEP
    assert n_pages % npg == 0
    width = k_pool.shape[2]

    def page_spec(i):
        return pl.BlockSpec((1, PAGE_SIZE, width), lambda b, j, pt: (pt[b * n_pages + j * npg + i], 0, 0))

    per_seq = lambda b, j, pt: (b, 0, 0)
    in_specs = [pl.BlockSpec((1, STREAM_ROWS, width), per_seq)]
    in_specs += [page_spec(i) for i in range(npg)] + [page_spec(i) for i in range(npg)]
    in_specs += [pl.BlockSpec((1, 1, width), per_seq), pl.BlockSpec((1, 1, width), per_seq)]
    args = [wq] + [k_pool] * npg + [v_pool] * npg + [k_new, v_new]
    if mode == "da":
        in_specs.append(pl.BlockSpec((4, DA_QK_DIM), lambda b, j, pt: (0, 0)))
        args.append(extra)
    else:
        in_specs.append(pl.BlockSpec((1, 1, npg * PAGE_SIZE), lambda b, j, pt: (b, 0, j)))
        in_specs.append(pl.BlockSpec((1, 1, PAGE_SIZE), lambda b, j, pt: (b, 0, n_pages)))
        args += [extra, extra]
    grid_spec = pltpu.PrefetchScalarGridSpec(
        num_scalar_prefetch=1,
        grid=(nb, n_pages // npg),
        in_specs=in_specs,
        out_specs=pl.BlockSpec((1, 1, 8 * LANES), per_seq),
        scratch_shapes=[pltpu.VMEM((STREAM_ROWS, 1), F32), pltpu.VMEM((STREAM_ROWS, 1), F32),
                        pltpu.VMEM((STREAM_ROWS, width), F32)],
    )
    return pl.pallas_call(
        functools.partial(_paged_attn_kernel, mode, lam_init, n_pages),
        out_shape=jax.ShapeDtypeStruct((nb, 1, 8 * LANES), F32),
        grid_spec=grid_spec,
        compiler_params=_cparams("parallel", "arbitrary"),
        name="paged_" + mode,
    )(page_table.reshape(-1), *args)


def _paged_index_kernel(*refs):
    npg = PAGES_PER_STEP
    q_ref, w_ref = refs[1], refs[2]
    k_refs = refs[3:3 + npg]
    o_ref = refs[3 + npg]
    k = jnp.concatenate([r[0] for r in k_refs], axis=0).astype(BF16)
    d = lax.dot_general(q_ref[0], k, (((1,), (1,)), ((), ())), preferred_element_type=F32)
    o_ref[0] = jnp.sum(jnp.maximum(d, 0.0) * w_ref[0], axis=0, keepdims=True)


def _paged_index(page_table, qi, wi, k_pool):
    nb, n_pages = page_table.shape
    npg = PAGES_PER_STEP

    def page_spec(i):
        return pl.BlockSpec((1, PAGE_SIZE, IDX_DIM), lambda b, j, pt: (pt[b * n_pages + j * npg + i], 0, 0))

    per_seq = lambda b, j, pt: (b, 0, 0)
    grid_spec = pltpu.PrefetchScalarGridSpec(
        num_scalar_prefetch=1,
        grid=(nb, n_pages // npg),
        in_specs=[pl.BlockSpec((1, IDX_HEADS, IDX_DIM), per_seq), pl.BlockSpec((1, IDX_HEADS, 1), per_seq)]
        + [page_spec(i) for i in range(npg)],
        out_specs=pl.BlockSpec((1, 1, npg * PAGE_SIZE), lambda b, j, pt: (b, 0, j)),
    )
    return pl.pallas_call(
        _paged_index_kernel,
        out_shape=jax.ShapeDtypeStruct((nb, 1, n_pages * PAGE_SIZE), F32),
        grid_spec=grid_spec,
        compiler_params=_cparams("parallel", "arbitrary"),
        name="paged_index",
    )(page_table.reshape(-1), qi, wi, *([k_pool] * npg))


def _sample_select_kernel(topk, sc_ref, qi_ref, kin_ref, wi_ref, o_ref, key_scr):
    nb, past = sc_ref.shape
    width = past + LANES
    prod = qi_ref[...] * kin_ref[...]
    lane = lax.broadcasted_iota(I32, (nb, LANES), 1)
    wi = wi_ref[...]
    s_new = jnp.zeros((nb, 1), F32)
    for h in range(IDX_HEADS):
        blk = prod[:, (h // 2) * LANES:(h // 2 + 1) * LANES]
        keep = (lane < IDX_DIM) if h % 2 == 0 else (lane >= IDX_DIM)
        d = jnp.sum(jnp.where(keep, blk, 0.0), axis=-1, keepdims=True)
        s_new = s_new + jnp.maximum(d, 0.0) * wi[:, h:h + 1]
    key_scr[:, 0:past] = _float_key(sc_ref[...])
    key_scr[:, past:width] = jnp.where(lane == 0, _float_key(s_new), INT_MIN)
    col = lax.broadcasted_iota(I32, (nb, width), 1)

    def count(pred):
        return jnp.sum(jnp.where(pred(key_scr[...], col), 1.0, 0.0), axis=-1, keepdims=True)

    kf = float(topk)
    t = _kth_largest_key(lambda cand: count(lambda keys, idx: keys >= cand), nb, kf)
    t = jnp.maximum(t, INT_MIN + 1)
    n_ge = count(lambda keys, idx: keys >= t)

    @pl.when(jnp.max(n_ge) > kf)
    def _():
        need = kf - count(lambda keys, idx: keys > t)

        def bit_body(i, x):
            cand = x + lax.shift_left(jnp.int32(1), 14 - i)
            below = count(lambda keys, idx: jnp.logical_and(keys == t, idx < cand))
            return jnp.where(below < need, cand, x)

        last = lax.fori_loop(0, 15, bit_body, jnp.zeros((nb, 1), I32))
        keys = key_scr[...]
        key_scr[...] = jnp.where(jnp.logical_and(keys == t, col > last), INT_MIN, keys)

    o_ref[...] = jnp.where(key_scr[...] >= t, 0.0, NEG).astype(BF16)


def _sample_select(scores, qi, ki_new8, wi, topk):
    nb, past = scores.shape
    assert past + LANES < 2 ** 15
    return pl.pallas_call(
        functools.partial(_sample_select_kernel, topk),
        out_shape=jax.ShapeDtypeStruct((nb, past + LANES), BF16),
        scratch_shapes=[pltpu.VMEM((nb, past + LANES), I32)],
        compiler_params=pltpu.CompilerParams(vmem_limit_bytes=VMEM_LIMIT_BYTES),
        name="sample_select",
    )(scores, qi, ki_new8, wi)


def _rope_tables(pos, d):
    r = d // 4
    half = r // 2
    inv = 1.0 / (ROPE_THETA ** (jnp.arange(half, dtype=F32) * (2.0 / r)))
    ang = pos.astype(F32)[:, None] * inv[None, :]
    cos, sin = jnp.cos(ang), jnp.sin(ang)
    n = pos.shape[0]
    c = jnp.concatenate([cos, cos, jnp.ones((n, d - r), F32)], axis=1)
    s = jnp.concatenate([-sin, sin, jnp.zeros((n, d - r), F32)], axis=1)
    reps = LANES // d
    return jnp.tile(c, (1, reps)), jnp.tile(s, (1, reps))


def _pack_w_in(w):
    parts = jnp.split(w, np.cumsum(SPLITS).tolist(), axis=1)
    qa, ka, va, qd, kd, vd, qi, ki, wi, gates = parts
    ga, gb = jnp.split(gates, 2, axis=1)
    pad = jnp.zeros((w.shape[0], PROJ_TN - IDX_DIM - IDX_HEADS), w.dtype)
    return jnp.concatenate([qa, ka, va, qd, kd, vd, ga, gb, qi, ki, wi, pad], axis=1).astype(BF16)


def _cols(z, name):
    off, width = _OFF[name]
    return z[:, off:off + width]


def _key_tiles(k, n_heads):
    t = k.shape[0]
    kt = k.astype(BF16).reshape(t // TK, TK, n_heads, LANES)
    return jnp.transpose(kt, (2, 0, 3, 1))


def _layer_weights(l, p):
    return dict(
        norm_ffn1=p["norm_ffn1"][l], w_gu1=p["ffn1_w_gu"][l].astype(BF16), w_d1=p["ffn1_w_down"][l].astype(BF16),
        norm_mix=p["norm_mix"][l], w_in=_pack_w_in(p["w_in"][l]),
        gains=tuple(jnp.tile(p[n][l], LANES // p[n].shape[1]).reshape(1, LANES)
                    for n in ("da_q_norm", "da_k_norm", "dsa_q_norm", "dsa_k_norm")),
        lp=p["da_lambda"][l], out_norm=p["da_out_norm"][l],
        wa=p["w_branch_a"][l].astype(BF16), wb=p["w_branch_b"][l].astype(BF16), wo=p["w_out"][l].astype(BF16),
        norm_ffn2=p["norm_ffn2"][l], w_gu2=p["ffn2_w_gu"][l].astype(BF16), w_d2=p["ffn2_w_down"][l].astype(BF16),
    )


def _prompt_layer(x, w, tabs, lam_init, topk):
    x = _ffn(x, w["norm_ffn1"], w["w_gu1"], w["w_d1"], TM_FFN)
    z = _proj(x, w["norm_mix"], w["w_in"], tabs, w["gains"], TM_PROJ)
    ka, va, kd, vd = _cols(z, "ka"), _cols(z, "va"), _cols(z, "kd"), _cols(z, "vd")
    kw = _cols(z, "kw")
    ki = kw[:, :IDX_DIM]
    qa = (_cols(z, "qa") * DA_QK_DIM ** -0.5).astype(BF16)
    oa = _diff_attn(qa, _key_tiles(ka, DA_KV_HEADS), va.astype(BF16), w["lp"], lam_init)
    ki2 = _key_tiles(jnp.concatenate([ki, ki], axis=1), 1)[0]
    wi = kw[:, IDX_DIM:IDX_DIM + LANES]
    bias = _select(_cols(z, "qi").astype(BF16), wi, ki2, topk)
    qd = (_cols(z, "qd") * DSA_HEAD_DIM ** -0.5).astype(BF16)
    od = _dsa_attn(qd, _key_tiles(kd, DSA_KV_HEADS), vd.astype(BF16), bias)
    x = _merge(x, oa, od, z, w["out_norm"], w["wa"], w["wb"], w["wo"], lam_init, TM_MERGE)
    x = _ffn(x, w["norm_ffn2"], w["w_gu2"], w["w_d2"], TM_FFN)
    return x, (ka, va, kd, vd, ki)


def _expand_queries(q, n_kv, per_kv_rows, width, scale):
    nb = q.shape[0]
    eye = jnp.eye(n_kv, dtype=q.dtype)
    out = jnp.einsum("bkrw,kc->bkrcw", q * scale, eye).reshape(nb, n_kv * per_kv_rows, n_kv * width)
    pad = STREAM_ROWS - n_kv * per_kv_rows
    return jnp.pad(out, ((0, 0), (0, pad), (0, 0))).astype(BF16)


def _sample_layer(x, w, tabs, lam_init, topk, caches, page_table):
    cache_da_k, cache_da_v, cache_dsa_k, cache_dsa_v, cache_idx_k = caches
    nb = page_table.shape[0]
    rows = x.shape[0]
    x = _ffn(x, w["norm_ffn1"], w["w_gu1"], w["w_d1"], rows)
    z = _proj(x, w["norm_mix"], w["w_in"], tabs, w["gains"], rows)
    ka, va, kd, vd = (_cols(z, n)[:nb] for n in ("ka", "va", "kd", "vd"))
    kw = _cols(z, "kw")[:nb]
    ki = kw[:, :IDX_DIM]
    wi = kw[:, IDX_DIM:IDX_DIM + IDX_HEADS]
    n_pool = cache_da_k.shape[0]
    qa = _cols(z, "qa")[:nb].reshape(nb, DA_KV_HEADS, 2, 2, DA_QK_DIM)
    blk = jnp.einsum("bkgmd,mn->bkgmnd", qa * DA_QK_DIM ** -0.5, jnp.eye(2, dtype=F32))
    wq_da = _expand_queries(blk.reshape(nb, DA_KV_HEADS, 4, 2 * DA_QK_DIM), DA_KV_HEADS, 4, LANES, 1.0)
    oa = _paged_attn("da", page_table, wq_da,
                     cache_da_k.reshape(n_pool, PAGE_SIZE, -1), cache_da_v.reshape(n_pool, PAGE_SIZE, -1),
                     ka.reshape(nb, 1, -1), va.reshape(nb, 1, -1), w["lp"], lam_init)
    qi = _cols(z, "qi")[:nb]
    scores = _paged_index(page_table, qi.reshape(nb, IDX_HEADS, IDX_DIM).astype(BF16),
                          wi.reshape(nb, IDX_HEADS, 1), cache_idx_k)
    bias = _sample_select(scores.reshape(nb, -1), qi, jnp.tile(ki, (1, IDX_HEADS)),
                          jnp.pad(wi, ((0, 0), (0, LANES - IDX_HEADS))), topk)
    qd = _cols(z, "qd")[:nb].reshape(nb, DSA_KV_HEADS, 2, DSA_HEAD_DIM)
    wq_dsa = _expand_queries(qd, DSA_KV_HEADS, 2, DSA_HEAD_DIM, DSA_HEAD_DIM ** -0.5)
    od = _paged_attn("dsa", page_table, wq_dsa,
                     cache_dsa_k.reshape(n_pool, PAGE_SIZE, -1), cache_dsa_v.reshape(n_pool, PAGE_SIZE, -1),
                     kd.reshape(nb, 1, -1), vd.reshape(nb, 1, -1), bias.reshape(nb, 1, -1), None)
    pad = ((0, rows - nb), (0, 0))
    x = _merge(x, jnp.pad(oa.reshape(nb, -1), pad), jnp.pad(od.reshape(nb, -1), pad), z,
               w["out_norm"], w["wa"], w["wb"], w["wo"], lam_init, rows)
    x = _ffn(x, w["norm_ffn2"], w["w_gu2"], w["w_d2"], rows)
    return x, (ka, va, kd, vd, ki)


def kernel(x_prompt, x_sample, cache_da_k, cache_da_v, cache_dsa_k, cache_dsa_v, cache_idx_k, page_table,
           meta_tokens, norm_ffn1, ffn1_w_gu, ffn1_w_down, norm_mix, w_in, da_q_norm, da_k_norm,
           dsa_q_norm, dsa_k_norm, da_lambda, da_out_norm, w_branch_a, w_branch_b, w_out,
           norm_ffn2, ffn2_w_gu, ffn2_w_down):
    params = dict(norm_ffn1=norm_ffn1, ffn1_w_gu=ffn1_w_gu, ffn1_w_down=ffn1_w_down, norm_mix=norm_mix,
                  w_in=w_in, da_q_norm=da_q_norm, da_k_norm=da_k_norm, dsa_q_norm=dsa_q_norm,
                  dsa_k_norm=dsa_k_norm, da_lambda=da_lambda, da_out_norm=da_out_norm,
                  w_branch_a=w_branch_a, w_branch_b=w_branch_b, w_out=w_out, norm_ffn2=norm_ffn2,
                  ffn2_w_gu=ffn2_w_gu, ffn2_w_down=ffn2_w_down)
    depth, d_model = norm_mix.shape
    assert w_in.shape[2] == sum(SPLITS) + 2 * d_model and d_model == DA_HEADS * DA_V_DIM
    assert x_prompt.shape[0] == 1 and x_sample.shape[1] == 1
    weights = [_layer_weights(l, params) for l in range(depth)]
    lam_inits = [0.8 - 0.6 * math.exp(-0.3 * l) for l in range(depth)]

    seq = x_prompt.shape[1]
    t = seq + N_META
    t_pad = -(-t // T_ALIGN) * T_ALIGN
    xp = jnp.concatenate([meta_tokens.astype(F32), x_prompt[0], jnp.zeros((t_pad - t, d_model), F32)], axis=0)
    pos_p = jnp.arange(t_pad, dtype=jnp.int32)
    tabs_p = _rope_tables(pos_p, DA_QK_DIM) + _rope_tables(pos_p, DSA_HEAD_DIM)
    topk_p = min(TOPK_MAX, t // 4)
    rows_p = []
    for l in range(depth):
        xp, rows = _prompt_layer(xp, weights[l], tabs_p, lam_inits[l], topk_p)
        rows_p.append(tuple(r[:t] for r in rows))
    y_prompt = xp[N_META:t][None]

    nb, n_pages = page_table.shape
    past = n_pages * PAGE_SIZE
    rows_s_pad = -(-nb // 16) * 16
    xs = jnp.pad(x_sample[:, 0, :], ((0, rows_s_pad - nb), (0, 0)))
    pos_s = jnp.full((rows_s_pad,), past, jnp.int32)
    tabs_s = _rope_tables(pos_s, DA_QK_DIM) + _rope_tables(pos_s, DSA_HEAD_DIM)
    topk_s = min(TOPK_MAX, (past + 1) // 4)
    rows_s = []
    for l in range(depth):
        caches = (cache_da_k[l], cache_da_v[l], cache_dsa_k[l], cache_dsa_v[l], cache_idx_k[l])
        xs, rows = _sample_layer(xs, weights[l], tabs_s, lam_inits[l], topk_s, caches, page_table)
        rows_s.append(rows)
    y_sample = xs[:nb][:, None, :]

    def stack(rows, i, shape):
        return jnp.stack([r[i] for r in rows], axis=0).reshape((depth,) + shape)

    lead_p, lead_s = (1, t), (nb, 1)
    return (y_prompt, y_sample,
            stack(rows_p, 0, lead_p + (DA_KV_HEADS, 2, DA_QK_DIM)), stack(rows_p, 1, lead_p + (DA_KV_HEADS, DA_V_DIM)),
            stack(rows_p, 2, lead_p + (DSA_KV_HEADS, DSA_HEAD_DIM)), stack(rows_p, 3, lead_p + (DSA_KV_HEADS, DSA_HEAD_DIM)),
            stack(rows_p, 4, lead_p + (IDX_DIM,)),
            stack(rows_s, 0, lead_s + (DA_KV_HEADS, 2, DA_QK_DIM)), stack(rows_s, 1, lead_s + (DA_KV_HEADS, DA_V_DIM)),
            stack(rows_s, 2, lead_s + (DSA_KV_HEADS, DSA_HEAD_DIM)), stack(rows_s, 3, lead_s + (DSA_KV_HEADS, DSA_HEAD_DIM)),
            stack(rows_s, 4, lead_s + (IDX_DIM,)))
```

```python
import functools
import math

import jax
import jax.numpy as jnp
import numpy as np
from jax import lax
from jax.experimental import pallas as pl
from jax.experimental.pallas import tpu as pltpu

F32 = jnp.float32
BF16 = jnp.bfloat16
I32 = jnp.int32

N_META = 16
ROPE_THETA = 500000.0
EPS = 1e-6
PAGE_SIZE = 128
DA_HEADS, DA_KV_HEADS, DA_QK_DIM, DA_V_DIM = 8, 4, 64, 128
DSA_HEADS, DSA_KV_HEADS, DSA_HEAD_DIM = 8, 4, 128
IDX_HEADS, IDX_DIM = 8, 64
TOPK_MAX = 256
SPLITS = (DA_HEADS * 2 * DA_QK_DIM, DA_KV_HEADS * 2 * DA_QK_DIM, DA_KV_HEADS * DA_V_DIM,
          DSA_HEADS * DSA_HEAD_DIM, DSA_KV_HEADS * DSA_HEAD_DIM, DSA_KV_HEADS * DSA_HEAD_DIM,
          IDX_HEADS * IDX_DIM, IDX_DIM, IDX_HEADS)

LANES = 128
MXU_COLS = 256
VMEM_LIMIT_BYTES = 56 * 1024 * 1024

PROJ_TN = MXU_COLS
_OFF = {}
_o = 0
for _name, _w in (("qa", 1024), ("ka", 512), ("va", 512), ("qd", 1024), ("kd", 512), ("vd", 512),
                  ("ga", 1024), ("gb", 1024), ("qi", 512), ("kw", PROJ_TN)):
    _OFF[_name] = (_o, _w)
    _o += _w
PROJ_N = _o

LOG2E = math.log2(math.e)
NEG = -1e30
INT_MIN = -2 ** 31

TQ = 512
TK = 512
TQ_SEL = 128
SHORTLIST = 12
T_ALIGN = 1536
TM_FFN = 768
TM_PROJ = 512
TM_MERGE = 512
FF_CHUNK = 256
PAGES_PER_STEP = 8


def _cparams(*sem):
    return pltpu.CompilerParams(dimension_semantics=sem, vmem_limit_bytes=VMEM_LIMIT_BYTES)


def _ffn_kernel(x_ref, g_ref, wg_ref, wu_ref, wd_ref, o_ref, h_scr):
    c = pl.program_id(1)

    @pl.when(c == 0)
    def _():
        x = x_ref[...]
        ms = jnp.mean(x * x, axis=-1, keepdims=True)
        h_scr[...] = ((x * lax.rsqrt(ms + EPS)) * g_ref[...]).astype(BF16)
        o_ref[...] = x

    h = h_scr[...]
    g = jnp.dot(h, wg_ref[...], preferred_element_type=F32)
    u = jnp.dot(h, wu_ref[...], preferred_element_type=F32)
    a = ((g * jax.nn.sigmoid(g)) * u).astype(BF16)
    o_ref[...] += 0.5 * jnp.dot(a, wd_ref[...], preferred_element_type=F32)


def _ffn(x, gain, w_gu, w_down, tm):
    m, d = x.shape
    d_ff = w_down.shape[0]
    n_c = d_ff // FF_CHUNK
    return pl.pallas_call(
        _ffn_kernel,
        out_shape=jax.ShapeDtypeStruct((m, d), F32),
        grid=(m // tm, n_c),
        in_specs=[
            pl.BlockSpec((tm, d), lambda i, c: (i, 0)),
            pl.BlockSpec((1, d), lambda i, c: (0, 0)),
            pl.BlockSpec((d, FF_CHUNK), lambda i, c: (0, c)),
            pl.BlockSpec((d, FF_CHUNK), lambda i, c: (0, c + n_c)),
            pl.BlockSpec((FF_CHUNK, d), lambda i, c: (c, 0)),
        ],
        out_specs=pl.BlockSpec((tm, d), lambda i, c: (i, 0)),
        scratch_shapes=[pltpu.VMEM((tm, d), BF16)],
        compiler_params=_cparams("parallel", "arbitrary"),
        name="ffn",
    )(x, gain.reshape(1, d), w_gu, w_gu, w_down)


def _seg_rms(x, gain, seg):
    cols = []
    for b in range(x.shape[1] // LANES):
        blk = x[:, b * LANES:(b + 1) * LANES]
        sq = blk * blk
        if seg == LANES:
            ms = jnp.sum(sq, axis=-1, keepdims=True) * (1.0 / seg)
        else:
            lane = lax.broadcasted_iota(I32, blk.shape, 1)
            lo = jnp.sum(jnp.where(lane < seg, sq, 0.0), axis=-1, keepdims=True)
            hi = jnp.sum(jnp.where(lane >= seg, sq, 0.0), axis=-1, keepdims=True)
            ms = jnp.where(lane < seg, lo, hi) * (1.0 / seg)
        cols.append((blk * lax.rsqrt(ms + EPS)) * gain)
    return jnp.concatenate(cols, axis=1)


def _rope(x, cos, sin, half):
    n = x.shape[1]
    reps = n // LANES
    c = jnp.concatenate([cos] * reps, axis=1)
    s = jnp.concatenate([sin] * reps, axis=1)
    up = pltpu.roll(x, n - half, 1)
    dn = pltpu.roll(x, half, 1)
    lane = lax.broadcasted_iota(I32, x.shape, 1)
    partner = jnp.where((lane & (2 * half - 1)) < half, up, dn)
    return x * c + partner * s


def _proj_kernel(x_ref, g_ref, w_ref, c64_ref, s64_ref, c128_ref, s128_ref,
                 qa_g, ka_g, qd_g, kd_g, o_ref, h_scr):
    j = pl.program_id(1)

    @pl.when(j == 0)
    def _():
        x = x_ref[...]
        ms = jnp.mean(x * x, axis=-1, keepdims=True)
        h_scr[...] = ((x * lax.rsqrt(ms + EPS)) * g_ref[...]).astype(BF16)

    z = jnp.dot(h_scr[...], w_ref[...], preferred_element_type=F32)

    def tile_of(name):
        off, width = _OFF[name]
        return off // PROJ_TN, (off + width) // PROJ_TN

    def in_group(name):
        lo, hi = tile_of(name)
        return jnp.logical_and(j >= lo, j < hi)

    h64, h128 = DA_QK_DIM // 8, DSA_HEAD_DIM // 8

    @pl.when(in_group("qa"))
    def _():
        o_ref[...] = _rope(_seg_rms(z, qa_g[...], DA_QK_DIM), c64_ref[...], s64_ref[...], h64)

    @pl.when(in_group("ka"))
    def _():
        o_ref[...] = _rope(_seg_rms(z, ka_g[...], DA_QK_DIM), c64_ref[...], s64_ref[...], h64)

    @pl.when(in_group("qd"))
    def _():
        o_ref[...] = _rope(_seg_rms(z, qd_g[...], DSA_HEAD_DIM), c128_ref[...], s128_ref[...], h128)

    @pl.when(in_group("kd"))
    def _():
        o_ref[...] = _rope(_seg_rms(z, kd_g[...], DSA_HEAD_DIM), c128_ref[...], s128_ref[...], h128)

    @pl.when(jnp.logical_or(in_group("va"), in_group("vd")))
    def _():
        o_ref[...] = z

    @pl.when(in_group("qi"))
    def _():
        o_ref[...] = _rope(z, c64_ref[...], s64_ref[...], h64)

    @pl.when(jnp.logical_or(in_group("ga"), in_group("gb")))
    def _():
        o_ref[...] = jax.nn.sigmoid(z)

    @pl.when(in_group("kw"))
    def _():
        r = _rope(z, c64_ref[...], s64_ref[...], h64)
        lane = lax.broadcasted_iota(I32, z.shape, 1)
        wi_scale = IDX_HEADS ** -0.5 * IDX_DIM ** -0.5
        o_ref[...] = jnp.where(lane < IDX_DIM, r, z * wi_scale)


def _proj(x, gain, w_packed, tabs, gains, tm):
    m, d = x.shape
    n = w_packed.shape[1]
    c64, s64, c128, s128 = tabs
    row = lambda i, j: (i, 0)
    const = lambda i, j: (0, 0)
    return pl.pallas_call(
        _proj_kernel,
        out_shape=jax.ShapeDtypeStruct((m, n), F32),
        grid=(m // tm, n // PROJ_TN),
        in_specs=[
            pl.BlockSpec((tm, d), row),
            pl.BlockSpec((1, d), const),
            pl.BlockSpec((d, PROJ_TN), lambda i, j: (0, j)),
            pl.BlockSpec((tm, LANES), row), pl.BlockSpec((tm, LANES), row),
            pl.BlockSpec((tm, LANES), row), pl.BlockSpec((tm, LANES), row),
            pl.BlockSpec((1, LANES), const), pl.BlockSpec((1, LANES), const),
            pl.BlockSpec((1, LANES), const), pl.BlockSpec((1, LANES), const),
        ],
        out_specs=pl.BlockSpec((tm, PROJ_TN), lambda i, j: (i, j)),
        scratch_shapes=[pltpu.VMEM((tm, d), BF16)],
        compiler_params=_cparams("parallel", "arbitrary"),
        name="proj",
    )(x, gain.reshape(1, d), w_packed, c64, s64, c128, s128, *gains)


def _merge_kernel(lam_init, x_ref, oa_ref, od_ref, ga_ref, gb_ref, g_ref, wa_ref, wb_ref, wo_ref,
                  o_ref, a_scr):
    for h in range(DA_HEADS):
        blk = oa_ref[:, h * DA_V_DIM:(h + 1) * DA_V_DIM]
        ms = jnp.mean(blk * blk, axis=-1, keepdims=True)
        y = ((blk * lax.rsqrt(ms + EPS)) * g_ref[...]) * (1.0 - lam_init)
        a_scr[:, h * DA_V_DIM:(h + 1) * DA_V_DIM] = y.astype(BF16)
    a = jnp.dot(a_scr[...], wa_ref[...], preferred_element_type=F32)
    b = jnp.dot(od_ref[...].astype(BF16), wb_ref[...], preferred_element_type=F32)
    merged = ga_ref[...] * a + gb_ref[...] * b
    o_ref[...] = x_ref[...] + jnp.dot(merged.astype(BF16), wo_ref[...], preferred_element_type=F32)


def _merge(x, oa, od, z, gain, wa, wb, wo, lam_init, tm):
    m, d = x.shape
    row = lambda i: (i, 0)
    const = lambda i: (0, 0)
    ga_blk = _OFF["ga"][0] // d
    gb_blk = _OFF["gb"][0] // d
    return pl.pallas_call(
        functools.partial(_merge_kernel, lam_init),
        out_shape=jax.ShapeDtypeStruct((m, d), F32),
        grid=(m // tm,),
        in_specs=[
            pl.BlockSpec((tm, d), row), pl.BlockSpec((tm, d), row), pl.BlockSpec((tm, d), row),
            pl.BlockSpec((tm, d), lambda i: (i, ga_blk)), pl.BlockSpec((tm, d), lambda i: (i, gb_blk)),
            pl.BlockSpec((1, DA_V_DIM), const),
            pl.BlockSpec((d, d), const), pl.BlockSpec((d, d), const), pl.BlockSpec((d, d), const),
        ],
        out_specs=pl.BlockSpec((tm, d), row),
        scratch_shapes=[pltpu.VMEM((tm, d), BF16)],
        compiler_params=_cparams("parallel"),
        name="merge",
    )(x, oa, od, z, z, gain.reshape(1, DA_V_DIM), wa, wb, wo)


def _lambda_of(lp, lam_init):
    a = jnp.sum(lp[0:1, :] * lp[1:2, :], axis=-1, keepdims=True)
    b = jnp.sum(lp[2:3, :] * lp[3:4, :], axis=-1, keepdims=True)
    return jnp.exp(a) - jnp.exp(b) + lam_init


def _softmax_step(s, v, m_scr, l_scr, acc_scr):
    m_prev = m_scr[...]
    m_new = jnp.maximum(m_prev, jnp.max(s, axis=-1, keepdims=True))
    alpha = jnp.exp(m_prev - m_new)
    p = jnp.exp(s - m_new)
    l_scr[...] = alpha * l_scr[...] + jnp.sum(p, axis=-1, keepdims=True)
    acc_scr[...] = alpha * acc_scr[...] + jnp.dot(p.astype(BF16), v, preferred_element_type=F32)
    m_scr[...] = m_new


def _flash_step(s, v_ext, m_ref, acc_ref):
    m_prev = m_ref[...]
    m_new = jnp.maximum(m_prev, jnp.max(s, axis=-1, keepdims=True))
    alpha = jnp.exp2(m_prev - m_new)
    p = jnp.exp2(s - pltpu.repeat(m_new, s.shape[1] // LANES, axis=1))
    acc_ref[...] = (pltpu.repeat(alpha, acc_ref.shape[1] // LANES, axis=1) * acc_ref[...]
                    + jnp.dot(p.astype(BF16), v_ext, preferred_element_type=F32))
    m_ref[...] = m_new


def _diff_attn_kernel(lam_init, q_ref, kt_ref, v_ref, lp_ref, o_ref, lhs_scr, m_scr, acc_scr):
    qt = pl.program_id(1)
    q = q_ref[...]
    lane = lax.broadcasted_iota(I32, (TQ, LANES), 1)
    for g in range(2):
        qg = q[:, g * LANES:(g + 1) * LANES]
        for mp in range(2):
            keep = (lane < DA_QK_DIM) if mp == 0 else (lane >= DA_QK_DIM)
            r = (g * 2 + mp) * TQ
            lhs_scr[r:r + TQ, :] = jnp.where(keep, qg, jnp.zeros_like(qg))
    m_scr[...] = jnp.full(m_scr.shape, NEG, F32)
    acc_scr[...] = jnp.zeros(acc_scr.shape, F32)
    rows = 4 * TQ

    def step(kt, masked):
        s = jnp.dot(lhs_scr[...], kt_ref[0, kt], preferred_element_type=F32)
        if masked:
            qpos = qt * TQ + (lax.broadcasted_iota(I32, (rows, TK), 0) & (TQ - 1))
            col = lax.broadcasted_iota(I32, (rows, TK), 1) + kt * TK
            s = jnp.where(col <= qpos, s, NEG)
        v = v_ref[pl.ds(pl.multiple_of(kt * TK, TK), TK), :]
        _flash_step(s, v, m_scr, acc_scr)

    def body(kt, carry):
        step(kt, False)
        return carry

    lax.fori_loop(0, qt, body, 0)
    step(qt, True)
    acc = acc_scr[...]
    o = acc[:, :DA_V_DIM] / acc[:, DA_V_DIM:]
    lam = _lambda_of(lp_ref[...], lam_init)
    for g in range(2):
        r = g * 2 * TQ
        o_ref[:, g * DA_V_DIM:(g + 1) * DA_V_DIM] = o[r:r + TQ, :] - lam * o[r + TQ:r + 2 * TQ, :]


def _diff_attn(q, kt, v_ext, lp, lam_init):
    t = q.shape[0]
    n_kt = t // TK
    assert TQ == TK
    return pl.pallas_call(
        functools.partial(_diff_attn_kernel, lam_init),
        out_shape=jax.ShapeDtypeStruct((t, DA_HEADS * DA_V_DIM), F32),
        grid=(DA_KV_HEADS, t // TQ),
        in_specs=[
            pl.BlockSpec((TQ, 2 * LANES), lambda kv, i: (i, kv)),
            pl.BlockSpec((1, n_kt, LANES, TK), lambda kv, i: (kv, 0, 0, 0)),
            pl.BlockSpec((t, 2 * DA_V_DIM), lambda kv, i: (0, kv)),
            pl.BlockSpec((4, DA_QK_DIM), lambda kv, i: (0, 0)),
        ],
        out_specs=pl.BlockSpec((TQ, 2 * DA_V_DIM), lambda kv, i: (i, kv)),
        scratch_shapes=[pltpu.VMEM((4 * TQ, LANES), BF16), pltpu.VMEM((4 * TQ, LANES), F32),
                        pltpu.VMEM((4 * TQ, 2 * DA_V_DIM), F32)],
        compiler_params=_cparams("parallel", "arbitrary"),
        name="diff_attn",
    )(q, kt, v_ext, lp)


def _float_key(s):
    s = jnp.where(s == 0.0, 0.0, s)
    bits = pltpu.bitcast(s, I32)
    return jnp.where(bits < 0, bits ^ jnp.int32(0x7FFFFFFF), bits)


def _key_float(key):
    bits = jnp.where(key < 0, key ^ jnp.int32(0x7FFFFFFF), key)
    return pltpu.bitcast(bits, F32)


KEY_LOWEST_FINITE = INT_MIN + 0x00800000


def _kth_largest_key(count_ge, rows, k):
    def bit_body(i, t):
        cand = t + lax.shift_left(jnp.int32(1), 31 - i)
        return jnp.where(count_ge(cand) >= k, cand, t)

    return lax.fori_loop(0, 32, bit_body, jnp.full((rows, 1), INT_MIN, I32))


def _select_kernel(topk, qi_ref, wi_ref, ki_ref, o_ref, lhs_scr, sc_scr, short_scr, t_scr, n_scr):
    qt = pl.program_id(0)
    n_kt = sc_scr.shape[0]
    n_k = ((qt + 1) * TQ_SEL + TK - 1) // TK
    lane = lax.broadcasted_iota(I32, (TQ_SEL, LANES), 1)
    for h in range(IDX_HEADS):
        pair = qi_ref[:, (h // 2) * LANES:(h // 2 + 1) * LANES]
        keep = (lane < IDX_DIM) if h % 2 == 0 else (lane >= IDX_DIM)
        lhs_scr[h * TQ_SEL:(h + 1) * TQ_SEL, :] = jnp.where(keep, pair, jnp.zeros_like(pair))
    wi = wi_ref[...]
    w_cols = [wi[:, h:h + 1] for h in range(IDX_HEADS)]
    qpos = qt * TQ_SEL + lax.broadcasted_iota(I32, (TQ_SEL, TK), 0)
    col = lax.broadcasted_iota(I32, (TQ_SEL, TK), 1)

    def score_body(kt, carry):
        d = jnp.dot(lhs_scr[...], ki_ref[kt], preferred_element_type=F32)
        sc = jnp.zeros((TQ_SEL, TK), F32)
        for h in range(IDX_HEADS):
            sc = sc + jnp.maximum(d[h * TQ_SEL:(h + 1) * TQ_SEL, :], 0.0) * w_cols[h]
        sc = jnp.where(sc == 0.0, 0.0, sc)
        sc_scr[kt] = jnp.where(col + kt * TK <= qpos, sc, -jnp.inf)
        return carry

    lax.fori_loop(0, n_k, score_body, 0)

    def count(pred):
        def body(kt, acc):
            hit = jnp.where(pred(sc_scr[kt], col + kt * TK), 1.0, 0.0)
            for c in range(TK // LANES):
                acc = acc + hit[:, c * LANES:(c + 1) * LANES]
            return acc

        acc = lax.fori_loop(0, n_k, body, jnp.zeros((TQ_SEL, LANES), F32))
        return jnp.sum(acc, axis=-1, keepdims=True)

    def shortlist_body(rg, carry):
        rows8 = pl.ds(pl.multiple_of(rg * 8, 8), 8)

        def insert_body(kt, tops):
            for c in range(TK // LANES):
                x = sc_scr[kt, rows8, c * LANES:(c + 1) * LANES]
                new = []
                for top in tops:
                    new.append(jnp.maximum(top, x))
                    x = jnp.minimum(top, x)
                tops = tuple(new)
            return tops

        tops = lax.fori_loop(0, n_k, insert_body, (jnp.full((8, LANES), -jnp.inf, F32),) * SHORTLIST)
        for j in range(SHORTLIST):
            short_scr[j, rows8, :] = _float_key(tops[j])
        return carry

    lax.fori_loop(0, TQ_SEL // 8, shortlist_body, 0)

    def count_short(cand):
        acc = jnp.zeros((TQ_SEL, LANES), F32)
        for j in range(SHORTLIST):
            acc = acc + jnp.where(short_scr[j] >= cand, 1.0, 0.0)
        return jnp.sum(acc, axis=-1, keepdims=True)

    kf = float(topk)
    t_key = jnp.maximum(_kth_largest_key(count_short, TQ_SEL, kf), KEY_LOWEST_FINITE)
    n_ge = count(lambda s, idx: s >= _key_float(t_key))
    t_scr[...] = t_key
    n_scr[...] = n_ge

    @pl.when(jnp.max(jnp.abs(n_ge - count_short(t_key))) > 0.0)
    def _():
        t_full = _kth_largest_key(lambda cand: count(lambda s, idx: s >= _key_float(cand)), TQ_SEL, kf)
        t_full = jnp.maximum(t_full, KEY_LOWEST_FINITE)
        t_scr[...] = t_full
        n_scr[...] = count(lambda s, idx: s >= _key_float(t_full))

    t = _key_float(t_scr[...])
    n_ge = n_scr[...]

    @pl.when(jnp.max(n_ge) > kf)
    def _():
        need = kf - count(lambda s, idx: s > t)

        def bit_body(i, x):
            cand = x + lax.shift_left(jnp.int32(1), 14 - i)
            below = count(lambda s, idx: jnp.logical_and(s == t, idx < cand))
            return jnp.where(below < need, cand, x)

        last = lax.fori_loop(0, 15, bit_body, jnp.zeros((TQ_SEL, 1), I32))

        def drop_body(kt, carry):
            s = sc_scr[kt]
            drop = jnp.logical_and(s == t, col + kt * TK > last)
            sc_scr[kt] = jnp.where(drop, -jnp.inf, s)
            return carry

        lax.fori_loop(0, n_k, drop_body, 0)

    def write_body(kt, carry):
        o_ref[kt] = jnp.where(sc_scr[kt] >= t, 0.0, NEG).astype(BF16)
        return carry

    lax.fori_loop(0, n_k, write_body, 0)

    def fill_body(kt, carry):
        o_ref[kt] = jnp.full((TQ_SEL, TK), NEG, BF16)
        return carry

    lax.fori_loop(n_k, n_kt, fill_body, 0)


def _select(qi, wi, ki2, topk):
    t = qi.shape[0]
    n_kt = t // TK
    assert t < 2 ** 15
    return pl.pallas_call(
        functools.partial(_select_kernel, topk),
        out_shape=jax.ShapeDtypeStruct((n_kt, t, TK), BF16),
        grid=(t // TQ_SEL,),
        in_specs=[
            pl.BlockSpec((TQ_SEL, IDX_HEADS * IDX_DIM), lambda i: (i, 0)),
            pl.BlockSpec((TQ_SEL, LANES), lambda i: (i, 0)),
            pl.BlockSpec((n_kt, LANES, TK), lambda i: (0, 0, 0)),
        ],
        out_specs=pl.BlockSpec((n_kt, TQ_SEL, TK), lambda i: (0, i, 0)),
        scratch_shapes=[pltpu.VMEM((IDX_HEADS * TQ_SEL, LANES), BF16),
                        pltpu.VMEM((n_kt, TQ_SEL, TK), F32),
                        pltpu.VMEM((SHORTLIST, TQ_SEL, LANES), I32),
                        pltpu.VMEM((TQ_SEL, 1), I32), pltpu.VMEM((TQ_SEL, 1), F32)],
        compiler_params=_cparams("parallel"),
        name="dsa_select",
    )(qi, wi, ki2)


def _dsa_attn_kernel(q_ref, kt_ref, v_ref, b_ref, o_ref, lhs_scr, m_scr, acc_scr):
    qt, kt = pl.program_id(0), pl.program_id(1)

    @pl.when(kt == 0)
    def _():
        for kv in range(DSA_KV_HEADS):
            for g in range(2):
                h = kv * 2 + g
                lhs_scr[kv, g * TQ:(g + 1) * TQ, :] = q_ref[:, h * LANES:(h + 1) * LANES]
        m_scr[...] = jnp.full(m_scr.shape, NEG, F32)
        acc_scr[...] = jnp.zeros(acc_scr.shape, F32)

    @pl.when(kt <= qt)
    def _():
        bias = b_ref[0].astype(F32)
        bias2 = jnp.concatenate([bias, bias], axis=0)
        for kv in range(DSA_KV_HEADS):
            s = jnp.dot(lhs_scr[kv], kt_ref[kv, 0], preferred_element_type=F32) + bias2
            v = v_ref[:, kv * 2 * DSA_HEAD_DIM:(kv + 1) * 2 * DSA_HEAD_DIM]
            _flash_step(s, v, m_scr.at[kv], acc_scr.at[kv])

    @pl.when(kt == qt)
    def _():
        for kv in range(DSA_KV_HEADS):
            acc = acc_scr[kv]
            o = acc[:, :DSA_HEAD_DIM] / acc[:, DSA_HEAD_DIM:]
            for g in range(2):
                h = kv * 2 + g
                o_ref[:, h * DSA_HEAD_DIM:(h + 1) * DSA_HEAD_DIM] = o[g * TQ:(g + 1) * TQ, :]


def _dsa_attn(q, kt, v_ext, bias):
    t = q.shape[0]
    assert TQ == TK
    kv, d = DSA_KV_HEADS, DSA_HEAD_DIM
    return pl.pallas_call(
        _dsa_attn_kernel,
        out_shape=jax.ShapeDtypeStruct((t, DSA_HEADS * d), F32),
        grid=(t // TQ, t // TK),
        in_specs=[
            pl.BlockSpec((TQ, DSA_HEADS * d), lambda i, j: (i, 0)),
            pl.BlockSpec((kv, 1, LANES, TK), lambda i, j: (0, jnp.minimum(i, j), 0, 0)),
            pl.BlockSpec((TK, kv * 2 * d), lambda i, j: (jnp.minimum(i, j), 0)),
            pl.BlockSpec((1, TQ, TK), lambda i, j: (jnp.minimum(i, j), i, 0)),
        ],
        out_specs=pl.BlockSpec((TQ, DSA_HEADS * d), lambda i, j: (i, 0)),
        scratch_shapes=[pltpu.VMEM((kv, 2 * TQ, LANES), BF16), pltpu.VMEM((kv, 2 * TQ, LANES), F32),
                        pltpu.VMEM((kv, 2 * TQ, 2 * d), F32)],
        compiler_params=_cparams("parallel", "arbitrary"),
        name="dsa_attn",
    )(q, kt, v_ext, bias)


STREAM_ROWS = 16


def _paged_attn_kernel(mode, lam_init, n_pages, *refs):
    npg = PAGES_PER_STEP
    pt_ref, wq_ref = refs[0], refs[1]
    k_refs = refs[2:2 + npg]
    v_refs = refs[2 + npg:2 + 2 * npg]
    knew_ref, vnew_ref = refs[2 + 2 * npg], refs[3 + 2 * npg]
    pos = 4 + 2 * npg
    if mode == "da":
        lp_ref = refs[pos]
        pos += 1
    else:
        bias_ref, bnew_ref = refs[pos], refs[pos + 1]
        pos += 2
    o_ref, m_scr, l_scr, acc_scr = refs[pos:pos + 4]
    j = pl.program_id(1)
    wq = wq_ref[0]

    @pl.when(j == 0)
    def _():
        s_new = jnp.sum(wq.astype(F32) * knew_ref[0], axis=-1, keepdims=True)
        if mode == "dsa":
            s_new = s_new + bnew_ref[0][:, 0:1].astype(F32)
        m_scr[...] = s_new
        l_scr[...] = jnp.ones(l_scr.shape, F32)
        acc_scr[...] = jnp.broadcast_to(vnew_ref[0], acc_scr.shape)

    def token_rows(ref):
        n_kv = ref.shape[2] // PAGE_SIZE
        return jnp.concatenate([ref[0, 0, pl.ds(kv, PAGE_SIZE, stride=n_kv), :] for kv in range(n_kv)], axis=1)

    v = jnp.concatenate([token_rows(r) for r in v_refs], axis=0).astype(BF16)
    if mode == "da":
        kt = jnp.concatenate([r[0, 0].reshape(-1, PAGE_SIZE) for r in k_refs], axis=1).astype(BF16)
        s = jnp.dot(wq, kt, preferred_element_type=F32)
    else:
        k = jnp.concatenate([token_rows(r) for r in k_refs], axis=0).astype(BF16)
        s = lax.dot_general(wq, k, (((1,), (1,)), ((), ())), preferred_element_type=F32)
        s = s + bias_ref[0].astype(F32)
    _softmax_step(s, v, m_scr, l_scr, acc_scr)

    @pl.when(j == n_pages // npg - 1)
    def _():
        o = acc_scr[...] / l_scr[...]
        if mode == "da":
            lam = _lambda_of(lp_ref[...], lam_init)
            for kv in range(DA_KV_HEADS):
                for g in range(2):
                    r = kv * 4 + g * 2
                    blk = slice(kv * DA_V_DIM, (kv + 1) * DA_V_DIM)
                    h = kv * 2 + g
                    o_ref[0, :, h * DA_V_DIM:(h + 1) * DA_V_DIM] = o[r:r + 1, blk] - lam * o[r + 1:r + 2, blk]
        else:
            for kv in range(DSA_KV_HEADS):
                for g in range(2):
                    r = kv * 2 + g
                    o_ref[0, :, r * DSA_HEAD_DIM:(r + 1) * DSA_HEAD_DIM] = (
                        o[r:r + 1, kv * DSA_HEAD_DIM:(kv + 1) * DSA_HEAD_DIM])


def _paged_attn(mode, layer, page_table, wq, k_cache, v_cache, k_new, v_new, extra, lam_init):
    nb, n_pages = page_table.shape
    npg = PAGES_PER_STEP
    assert n_pages % npg == 0
    width = k_new.shape[2]

    def page_spec(cache, i):
        tail = cache.shape[2:]
        zeros = (0,) * len(tail)
        return pl.BlockSpec((1, 1) + tail, lambda b, j, pt: (layer, pt[b * n_pages + j * npg + i]) + zeros)

    per_seq = lambda b, j, pt: (b, 0, 0)
    in_specs = [pl.BlockSpec((1, STREAM_ROWS, width), per_seq)]
    in_specs += [page_spec(k_cache, i) for i in range(npg)] + [page_spec(v_cache, i) for i in range(npg)]
    in_specs += [pl.BlockSpec((1, 1, width), per_seq), pl.BlockSpec((1, 1, width), per_seq)]
    args = [wq] + [k_cache] * npg + [v_cache] * npg + [k_new, v_new]
    if mode == "da":
        in_specs.append(pl.BlockSpec((4, DA_QK_DIM), lambda b, j, pt: (0, 0)))
        args.append(extra)
    else:
        in_specs.append(pl.BlockSpec((1, 1, npg * PAGE_SIZE), lambda b, j, pt: (b, 0, j)))
        in_specs.append(pl.BlockSpec((1, 1, PAGE_SIZE), lambda b, j, pt: (b, 0, n_pages)))
        args += [extra, extra]
    grid_spec = pltpu.PrefetchScalarGridSpec(
        num_scalar_prefetch=1,
        grid=(nb, n_pages // npg),
        in_specs=in_specs,
        out_specs=pl.BlockSpec((1, 1, 8 * LANES), per_seq),
        scratch_shapes=[pltpu.VMEM((STREAM_ROWS, 1), F32), pltpu.VMEM((STREAM_ROWS, 1), F32),
                        pltpu.VMEM((STREAM_ROWS, width), F32)],
    )
    return pl.pallas_call(
        functools.partial(_paged_attn_kernel, mode, lam_init, n_pages),
        out_shape=jax.ShapeDtypeStruct((nb, 1, 8 * LANES), F32),
        grid_spec=grid_spec,
        compiler_params=_cparams("parallel", "arbitrary"),
        name="paged_" + mode,
    )(page_table.reshape(-1), *args)


def _paged_index_kernel(*refs):
    npg = PAGES_PER_STEP
    q_ref, w_ref = refs[1], refs[2]
    k_refs = refs[3:3 + npg]
    o_ref = refs[3 + npg]
    kt = jnp.concatenate([r[0, 0] for r in k_refs], axis=1).astype(BF16)
    d = jnp.dot(q_ref[0], kt, preferred_element_type=F32)
    o_ref[0] = jnp.sum(jnp.maximum(d, 0.0) * w_ref[0], axis=0, keepdims=True)


def _paged_index(layer, page_table, qi, wi, k_cache):
    nb, n_pages = page_table.shape
    npg = PAGES_PER_STEP

    def page_spec(i):
        return pl.BlockSpec((1, 1, IDX_DIM, PAGE_SIZE),
                            lambda b, j, pt: (layer, pt[b * n_pages + j * npg + i], 0, 0))

    per_seq = lambda b, j, pt: (b, 0, 0)
    grid_spec = pltpu.PrefetchScalarGridSpec(
        num_scalar_prefetch=1,
        grid=(nb, n_pages // npg),
        in_specs=[pl.BlockSpec((1, IDX_HEADS, IDX_DIM), per_seq), pl.BlockSpec((1, IDX_HEADS, 1), per_seq)]
        + [page_spec(i) for i in range(npg)],
        out_specs=pl.BlockSpec((1, 1, npg * PAGE_SIZE), lambda b, j, pt: (b, 0, j)),
    )
    return pl.pallas_call(
        _paged_index_kernel,
        out_shape=jax.ShapeDtypeStruct((nb, 1, n_pages * PAGE_SIZE), F32),
        grid_spec=grid_spec,
        compiler_params=_cparams("parallel", "arbitrary"),
        name="paged_index",
    )(page_table.reshape(-1), qi, wi, *([k_cache] * npg))


def _sample_select_kernel(topk, sc_ref, qi_ref, kin_ref, wi_ref, o_ref, key_scr):
    nb, past = sc_ref.shape
    width = past + LANES
    prod = qi_ref[...] * kin_ref[...]
    lane = lax.broadcasted_iota(I32, (nb, LANES), 1)
    wi = wi_ref[...]
    s_new = jnp.zeros((nb, 1), F32)
    for h in range(IDX_HEADS):
        blk = prod[:, (h // 2) * LANES:(h // 2 + 1) * LANES]
        keep = (lane < IDX_DIM) if h % 2 == 0 else (lane >= IDX_DIM)
        d = jnp.sum(jnp.where(keep, blk, 0.0), axis=-1, keepdims=True)
        s_new = s_new + jnp.maximum(d, 0.0) * wi[:, h:h + 1]
    key_scr[:, 0:past] = _float_key(sc_ref[...])
    key_scr[:, past:width] = jnp.where(lane == 0, _float_key(s_new), INT_MIN)
    col = lax.broadcasted_iota(I32, (nb, width), 1)

    def count(pred):
        return jnp.sum(jnp.where(pred(key_scr[...], col), 1.0, 0.0), axis=-1, keepdims=True)

    kf = float(topk)
    t = _kth_largest_key(lambda cand: count(lambda keys, idx: keys >= cand), nb, kf)
    t = jnp.maximum(t, INT_MIN + 1)
    n_ge = count(lambda keys, idx: keys >= t)

    @pl.when(jnp.max(n_ge) > kf)
    def _():
        need = kf - count(lambda keys, idx: keys > t)

        def bit_body(i, x):
            cand = x + lax.shift_left(jnp.int32(1), 14 - i)
            below = count(lambda keys, idx: jnp.logical_and(keys == t, idx < cand))
            return jnp.where(below < need, cand, x)

        last = lax.fori_loop(0, 15, bit_body, jnp.zeros((nb, 1), I32))
        keys = key_scr[...]
        key_scr[...] = jnp.where(jnp.logical_and(keys == t, col > last), INT_MIN, keys)

    o_ref[...] = jnp.where(key_scr[...] >= t, 0.0, NEG).astype(BF16)


def _sample_select(scores, qi, ki_new8, wi, topk):
    nb, past = scores.shape
    assert past + LANES < 2 ** 15
    return pl.pallas_call(
        functools.partial(_sample_select_kernel, topk),
        out_shape=jax.ShapeDtypeStruct((nb, past + LANES), BF16),
        scratch_shapes=[pltpu.VMEM((nb, past + LANES), I32)],
        compiler_params=pltpu.CompilerParams(vmem_limit_bytes=VMEM_LIMIT_BYTES),
        name="sample_select",
    )(scores, qi, ki_new8, wi)


def _rope_tables(pos, d):
    r = d // 4
    half = r // 2
    inv = 1.0 / (ROPE_THETA ** (jnp.arange(half, dtype=F32) * (2.0 / r)))
    ang = pos.astype(F32)[:, None] * inv[None, :]
    cos, sin = jnp.cos(ang), jnp.sin(ang)
    n = pos.shape[0]
    c = jnp.concatenate([cos, cos, jnp.ones((n, d - r), F32)], axis=1)
    s = jnp.concatenate([-sin, sin, jnp.zeros((n, d - r), F32)], axis=1)
    reps = LANES // d
    return jnp.tile(c, (1, reps)), jnp.tile(s, (1, reps))


def _pack_w_in(w):
    parts = jnp.split(w, np.cumsum(SPLITS).tolist(), axis=1)
    qa, ka, va, qd, kd, vd, qi, ki, wi, gates = parts
    ga, gb = jnp.split(gates, 2, axis=1)
    pad = jnp.zeros((w.shape[0], PROJ_TN - IDX_DIM - IDX_HEADS), w.dtype)
    return jnp.concatenate([qa, ka, va, qd, kd, vd, ga, gb, qi, ki, wi, pad], axis=1).astype(BF16)


def _cols(z, name):
    off, width = _OFF[name]
    return z[:, off:off + width]


def _key_tiles(k, n_heads):
    t = k.shape[0]
    kt = k.astype(BF16).reshape(t // TK, TK, n_heads, LANES)
    return jnp.transpose(kt, (2, 0, 3, 1))


def _with_ones(v, n_heads):
    t = v.shape[0]
    v3 = v.astype(BF16).reshape(t, n_heads, LANES)
    return jnp.concatenate([v3, jnp.ones_like(v3)], axis=2).reshape(t, n_heads * 2 * LANES)


def _layer_weights(l, p):
    return dict(
        norm_ffn1=p["norm_ffn1"][l], w_gu1=p["ffn1_w_gu"][l].astype(BF16), w_d1=p["ffn1_w_down"][l].astype(BF16),
        norm_mix=p["norm_mix"][l], w_in=_pack_w_in(p["w_in"][l]),
        gains=tuple(jnp.tile(p[n][l], LANES // p[n].shape[1]).reshape(1, LANES)
                    for n in ("da_q_norm", "da_k_norm", "dsa_q_norm", "dsa_k_norm")),
        lp=p["da_lambda"][l], out_norm=p["da_out_norm"][l],
        wa=p["w_branch_a"][l].astype(BF16), wb=p["w_branch_b"][l].astype(BF16), wo=p["w_out"][l].astype(BF16),
        norm_ffn2=p["norm_ffn2"][l], w_gu2=p["ffn2_w_gu"][l].astype(BF16), w_d2=p["ffn2_w_down"][l].astype(BF16),
    )


def _prompt_layer(x, w, tabs, lam_init, topk):
    x = _ffn(x, w["norm_ffn1"], w["w_gu1"], w["w_d1"], TM_FFN)
    z = _proj(x, w["norm_mix"], w["w_in"], tabs, w["gains"], TM_PROJ)
    ka, va, kd, vd = _cols(z, "ka"), _cols(z, "va"), _cols(z, "kd"), _cols(z, "vd")
    kw = _cols(z, "kw")
    ki = kw[:, :IDX_DIM]
    qa = (_cols(z, "qa") * (DA_QK_DIM ** -0.5 * LOG2E)).astype(BF16)
    oa = _diff_attn(qa, _key_tiles(ka, DA_KV_HEADS), _with_ones(va, DA_KV_HEADS), w["lp"], lam_init)
    ki2 = _key_tiles(jnp.concatenate([ki, ki], axis=1), 1)[0]
    wi = kw[:, IDX_DIM:IDX_DIM + LANES]
    bias = _select(_cols(z, "qi").astype(BF16), wi, ki2, topk)
    qd = (_cols(z, "qd") * (DSA_HEAD_DIM ** -0.5 * LOG2E)).astype(BF16)
    od = _dsa_attn(qd, _key_tiles(kd, DSA_KV_HEADS), _with_ones(vd, DSA_KV_HEADS), bias)
    x = _merge(x, oa, od, z, w["out_norm"], w["wa"], w["wb"], w["wo"], lam_init, TM_MERGE)
    x = _ffn(x, w["norm_ffn2"], w["w_gu2"], w["w_d2"], TM_FFN)
    return x, (ka, va, kd, vd, ki)


def _expand_queries(q, n_kv, per_kv_rows, width, scale):
    nb = q.shape[0]
    eye = jnp.eye(n_kv, dtype=q.dtype)
    out = jnp.einsum("bkrw,kc->bkrcw", q * scale, eye).reshape(nb, n_kv * per_kv_rows, n_kv * width)
    pad = STREAM_ROWS - n_kv * per_kv_rows
    return jnp.pad(out, ((0, 0), (0, pad), (0, 0))).astype(BF16)


def _cache_views(cache_da_k, cache_da_v, cache_dsa_k, cache_dsa_v, cache_idx_k):
    depth, n_pool = cache_da_k.shape[:2]
    rows = lambda c: c.reshape(depth, n_pool, -1, c.shape[-1])
    return (jnp.transpose(cache_da_k, (0, 1, 3, 4, 5, 2)), rows(cache_da_v), rows(cache_dsa_k),
            rows(cache_dsa_v), jnp.transpose(cache_idx_k, (0, 1, 3, 2)))


def _sample_layer(x, w, tabs, lam_init, topk, layer, caches, page_table):
    da_kt, da_v, dsa_k, dsa_v, idx_kt = caches
    nb = page_table.shape[0]
    rows = x.shape[0]
    x = _ffn(x, w["norm_ffn1"], w["w_gu1"], w["w_d1"], rows)
    z = _proj(x, w["norm_mix"], w["w_in"], tabs, w["gains"], rows)
    ka, va, kd, vd = (_cols(z, n)[:nb] for n in ("ka", "va", "kd", "vd"))
    kw = _cols(z, "kw")[:nb]
    ki = kw[:, :IDX_DIM]
    wi = kw[:, IDX_DIM:IDX_DIM + IDX_HEADS]
    qa = _cols(z, "qa")[:nb].reshape(nb, DA_KV_HEADS, 2, 2, DA_QK_DIM)
    blk = jnp.einsum("bkgmd,mn->bkgmnd", qa * DA_QK_DIM ** -0.5, jnp.eye(2, dtype=F32))
    wq_da = _expand_queries(blk.reshape(nb, DA_KV_HEADS, 4, 2 * DA_QK_DIM), DA_KV_HEADS, 4, LANES, 1.0)
    oa = _paged_attn("da", layer, page_table, wq_da, da_kt, da_v,
                     ka.reshape(nb, 1, -1), va.reshape(nb, 1, -1), w["lp"], lam_init)
    qi = _cols(z, "qi")[:nb]
    scores = _paged_index(layer, page_table, qi.reshape(nb, IDX_HEADS, IDX_DIM).astype(BF16),
                          wi.reshape(nb, IDX_HEADS, 1), idx_kt)
    bias = _sample_select(scores.reshape(nb, -1), qi, jnp.tile(ki, (1, IDX_HEADS)),
                          jnp.pad(wi, ((0, 0), (0, LANES - IDX_HEADS))), topk)
    qd = _cols(z, "qd")[:nb].reshape(nb, DSA_KV_HEADS, 2, DSA_HEAD_DIM)
    wq_dsa = _expand_queries(qd, DSA_KV_HEADS, 2, DSA_HEAD_DIM, DSA_HEAD_DIM ** -0.5)
    od = _paged_attn("dsa", layer, page_table, wq_dsa, dsa_k, dsa_v,
                     kd.reshape(nb, 1, -1), vd.reshape(nb, 1, -1), bias.reshape(nb, 1, -1), None)
    pad = ((0, rows - nb), (0, 0))
    x = _merge(x, jnp.pad(oa.reshape(nb, -1), pad), jnp.pad(od.reshape(nb, -1), pad), z,
               w["out_norm"], w["wa"], w["wb"], w["wo"], lam_init, rows)
    x = _ffn(x, w["norm_ffn2"], w["w_gu2"], w["w_d2"], rows)
    return x, (ka, va, kd, vd, ki)


def kernel(x_prompt, x_sample, cache_da_k, cache_da_v, cache_dsa_k, cache_dsa_v, cache_idx_k, page_table,
           meta_tokens, norm_ffn1, ffn1_w_gu, ffn1_w_down, norm_mix, w_in, da_q_norm, da_k_norm,
           dsa_q_norm, dsa_k_norm, da_lambda, da_out_norm, w_branch_a, w_branch_b, w_out,
           norm_ffn2, ffn2_w_gu, ffn2_w_down):
    params = dict(norm_ffn1=norm_ffn1, ffn1_w_gu=ffn1_w_gu, ffn1_w_down=ffn1_w_down, norm_mix=norm_mix,
                  w_in=w_in, da_q_norm=da_q_norm, da_k_norm=da_k_norm, dsa_q_norm=dsa_q_norm,
                  dsa_k_norm=dsa_k_norm, da_lambda=da_lambda, da_out_norm=da_out_norm,
                  w_branch_a=w_branch_a, w_branch_b=w_branch_b, w_out=w_out, norm_ffn2=norm_ffn2,
                  ffn2_w_gu=ffn2_w_gu, ffn2_w_down=ffn2_w_down)
    depth, d_model = norm_mix.shape
    assert w_in.shape[2] == sum(SPLITS) + 2 * d_model and d_model == DA_HEADS * DA_V_DIM
    assert x_prompt.shape[0] == 1 and x_sample.shape[1] == 1
    weights = [_layer_weights(l, params) for l in range(depth)]
    lam_inits = [0.8 - 0.6 * math.exp(-0.3 * l) for l in range(depth)]

    seq = x_prompt.shape[1]
    t = seq + N_META
    t_pad = -(-t // T_ALIGN) * T_ALIGN
    xp = jnp.concatenate([meta_tokens.astype(F32), x_prompt[0], jnp.zeros((t_pad - t, d_model), F32)], axis=0)
    pos_p = jnp.arange(t_pad, dtype=jnp.int32)
    tabs_p = _rope_tables(pos_p, DA_QK_DIM) + _rope_tables(pos_p, DSA_HEAD_DIM)
    topk_p = min(TOPK_MAX, t // 4)
    rows_p = []
    for l in range(depth):
        xp, rows = _prompt_layer(xp, weights[l], tabs_p, lam_inits[l], topk_p)
        rows_p.append(tuple(r[:t] for r in rows))
    y_prompt = xp[N_META:t][None]

    nb, n_pages = page_table.shape
    past = n_pages * PAGE_SIZE
    rows_s_pad = -(-nb // 16) * 16
    xs = jnp.pad(x_sample[:, 0, :], ((0, rows_s_pad - nb), (0, 0)))
    pos_s = jnp.full((rows_s_pad,), past, jnp.int32)
    tabs_s = _rope_tables(pos_s, DA_QK_DIM) + _rope_tables(pos_s, DSA_HEAD_DIM)
    topk_s = min(TOPK_MAX, (past + 1) // 4)
    rows_s = []
    caches = _cache_views(cache_da_k, cache_da_v, cache_dsa_k, cache_dsa_v, cache_idx_k)
    for l in range(depth):
        xs, rows = _sample_layer(xs, weights[l], tabs_s, lam_inits[l], topk_s, l, caches, page_table)
        rows_s.append(rows)
    y_sample = xs[:nb][:, None, :]

    def stack(rows, i, shape):
        return jnp.stack([r[i] for r in rows], axis=0).reshape((depth,) + shape)

    lead_p, lead_s = (1, t), (nb, 1)
    return (y_prompt, y_sample,
            stack(rows_p, 0, lead_p + (DA_KV_HEADS, 2, DA_QK_DIM)), stack(rows_p, 1, lead_p + (DA_KV_HEADS, DA_V_DIM)),
            stack(rows_p, 2, lead_p + (DSA_KV_HEADS, DSA_HEAD_DIM)), stack(rows_p, 3, lead_p + (DSA_KV_HEADS, DSA_HEAD_DIM)),
            stack(rows_p, 4, lead_p + (IDX_DIM,)),
            stack(rows_s, 0, lead_s + (DA_KV_HEADS, 2, DA_QK_DIM)), stack(rows_s, 1, lead_s + (DA_KV_HEADS, DA_V_DIM)),
            stack(rows_s, 2, lead_s + (DSA_KV_HEADS, DSA_HEAD_DIM)), stack(rows_s, 3, lead_s + (DSA_KV_HEADS, DSA_HEAD_DIM)),
            stack(rows_s, 4, lead_s + (IDX_DIM,)))
```

```python
import functools
import math

import jax
import jax.numpy as jnp
import numpy as np
from jax import lax
from jax.experimental import pallas as pl
from jax.experimental.pallas import tpu as pltpu

F32 = jnp.float32
BF16 = jnp.bfloat16
I32 = jnp.int32

N_META = 16
ROPE_THETA = 500000.0
EPS = 1e-6
PAGE_SIZE = 128
DA_HEADS, DA_KV_HEADS, DA_QK_DIM, DA_V_DIM = 8, 4, 64, 128
DSA_HEADS, DSA_KV_HEADS, DSA_HEAD_DIM = 8, 4, 128
IDX_HEADS, IDX_DIM = 8, 64
TOPK_MAX = 256
SPLITS = (DA_HEADS * 2 * DA_QK_DIM, DA_KV_HEADS * 2 * DA_QK_DIM, DA_KV_HEADS * DA_V_DIM,
          DSA_HEADS * DSA_HEAD_DIM, DSA_KV_HEADS * DSA_HEAD_DIM, DSA_KV_HEADS * DSA_HEAD_DIM,
          IDX_HEADS * IDX_DIM, IDX_DIM, IDX_HEADS)

LANES = 128
MXU_COLS = 256
VMEM_LIMIT_BYTES = 56 * 1024 * 1024

PROJ_TN = 2 * MXU_COLS
_OFF = {}
_o = 0
for _name, _w in (("qa", 1024), ("ka", 512), ("va", 512), ("qd", 1024), ("kd", 512), ("vd", 512),
                  ("ga", 1024), ("gb", 1024), ("qi", 512), ("kw", PROJ_TN)):
    _OFF[_name] = (_o, _w)
    _o += _w
PROJ_N = _o

LOG2E = math.log2(math.e)
NEG = -1e30
INT_MIN = -2 ** 31

TQ = 512
TK = 512
TQ_SEL = 128
SHORTLIST = 12
T_ALIGN = 1536
TM_FFN = 768
TM_PROJ = 768
TM_MERGE = 512
FF_CHUNK = 256
PAGES_PER_STEP = 16
INDEX_PAGES_PER_STEP = 32


def _cparams(*sem):
    return pltpu.CompilerParams(dimension_semantics=sem, vmem_limit_bytes=VMEM_LIMIT_BYTES)


def _ffn_kernel(x_ref, g_ref, wg_ref, wu_ref, wd_ref, o_ref, h_scr):
    c = pl.program_id(1)

    @pl.when(c == 0)
    def _():
        x = x_ref[...]
        ms = jnp.mean(x * x, axis=-1, keepdims=True)
        h_scr[...] = ((x * lax.rsqrt(ms + EPS)) * g_ref[...]).astype(BF16)
        o_ref[...] = x

    h = h_scr[...]
    g = jnp.dot(h, wg_ref[...], preferred_element_type=F32)
    u = jnp.dot(h, wu_ref[...], preferred_element_type=F32)
    a = ((g * jax.nn.sigmoid(g)) * u).astype(BF16)
    o_ref[...] += 0.5 * jnp.dot(a, wd_ref[...], preferred_element_type=F32)


def _ffn(x, gain, w_gu, w_down, tm):
    m, d = x.shape
    d_ff = w_down.shape[0]
    n_c = d_ff // FF_CHUNK
    return pl.pallas_call(
        _ffn_kernel,
        out_shape=jax.ShapeDtypeStruct((m, d), F32),
        grid=(m // tm, n_c),
        in_specs=[
            pl.BlockSpec((tm, d), lambda i, c: (i, 0)),
            pl.BlockSpec((1, d), lambda i, c: (0, 0)),
            pl.BlockSpec((d, FF_CHUNK), lambda i, c: (0, c)),
            pl.BlockSpec((d, FF_CHUNK), lambda i, c: (0, c + n_c)),
            pl.BlockSpec((FF_CHUNK, d), lambda i, c: (c, 0)),
        ],
        out_specs=pl.BlockSpec((tm, d), lambda i, c: (i, 0)),
        scratch_shapes=[pltpu.VMEM((tm, d), BF16)],
        compiler_params=_cparams("parallel", "arbitrary"),
        name="ffn",
    )(x, gain.reshape(1, d), w_gu, w_gu, w_down)


def _seg_rms(x, gain, seg):
    cols = []
    for b in range(x.shape[1] // LANES):
        blk = x[:, b * LANES:(b + 1) * LANES]
        sq = blk * blk
        if seg == LANES:
            ms = jnp.sum(sq, axis=-1, keepdims=True) * (1.0 / seg)
        else:
            lane = lax.broadcasted_iota(I32, blk.shape, 1)
            lo = jnp.sum(jnp.where(lane < seg, sq, 0.0), axis=-1, keepdims=True)
            hi = jnp.sum(jnp.where(lane >= seg, sq, 0.0), axis=-1, keepdims=True)
            ms = jnp.where(lane < seg, lo, hi) * (1.0 / seg)
        cols.append((blk * lax.rsqrt(ms + EPS)) * gain)
    return jnp.concatenate(cols, axis=1)


def _rope(x, cos, sin, half):
    n = x.shape[1]
    reps = n // LANES
    c = jnp.concatenate([cos] * reps, axis=1)
    s = jnp.concatenate([sin] * reps, axis=1)
    up = pltpu.roll(x, n - half, 1)
    dn = pltpu.roll(x, half, 1)
    lane = lax.broadcasted_iota(I32, x.shape, 1)
    partner = jnp.where((lane & (2 * half - 1)) < half, up, dn)
    return x * c + partner * s


def _proj_kernel(x_ref, g_ref, w_ref, c64_ref, s64_ref, c128_ref, s128_ref,
                 qa_g, ka_g, qd_g, kd_g, o_ref, h_scr):
    j = pl.program_id(1)

    @pl.when(j == 0)
    def _():
        x = x_ref[...]
        ms = jnp.mean(x * x, axis=-1, keepdims=True)
        h_scr[...] = ((x * lax.rsqrt(ms + EPS)) * g_ref[...]).astype(BF16)

    z = jnp.dot(h_scr[...], w_ref[...], preferred_element_type=F32)

    def tile_of(name):
        off, width = _OFF[name]
        return off // PROJ_TN, (off + width) // PROJ_TN

    def in_group(name):
        lo, hi = tile_of(name)
        return jnp.logical_and(j >= lo, j < hi)

    h64, h128 = DA_QK_DIM // 8, DSA_HEAD_DIM // 8

    @pl.when(in_group("qa"))
    def _():
        o_ref[...] = _rope(_seg_rms(z, qa_g[...], DA_QK_DIM), c64_ref[...], s64_ref[...], h64)

    @pl.when(in_group("ka"))
    def _():
        o_ref[...] = _rope(_seg_rms(z, ka_g[...], DA_QK_DIM), c64_ref[...], s64_ref[...], h64)

    @pl.when(in_group("qd"))
    def _():
        o_ref[...] = _rope(_seg_rms(z, qd_g[...], DSA_HEAD_DIM), c128_ref[...], s128_ref[...], h128)

    @pl.when(in_group("kd"))
    def _():
        o_ref[...] = _rope(_seg_rms(z, kd_g[...], DSA_HEAD_DIM), c128_ref[...], s128_ref[...], h128)

    @pl.when(jnp.logical_or(in_group("va"), in_group("vd")))
    def _():
        o_ref[...] = z

    @pl.when(in_group("qi"))
    def _():
        o_ref[...] = _rope(z, c64_ref[...], s64_ref[...], h64)

    @pl.when(jnp.logical_or(in_group("ga"), in_group("gb")))
    def _():
        o_ref[...] = jax.nn.sigmoid(z)

    @pl.when(in_group("kw"))
    def _():
        r = _rope(z, c64_ref[...], s64_ref[...], h64)
        lane = lax.broadcasted_iota(I32, z.shape, 1)
        wi_scale = IDX_HEADS ** -0.5 * IDX_DIM ** -0.5
        o_ref[...] = jnp.where(lane < IDX_DIM, r, z * wi_scale)


def _proj(x, gain, w_packed, tabs, gains, tm):
    m, d = x.shape
    n = w_packed.shape[1]
    c64, s64, c128, s128 = tabs
    row = lambda i, j: (i, 0)
    const = lambda i, j: (0, 0)
    return pl.pallas_call(
        _proj_kernel,
        out_shape=jax.ShapeDtypeStruct((m, n), F32),
        grid=(m // tm, n // PROJ_TN),
        in_specs=[
            pl.BlockSpec((tm, d), row),
            pl.BlockSpec((1, d), const),
            pl.BlockSpec((d, PROJ_TN), lambda i, j: (0, j)),
            pl.BlockSpec((tm, LANES), row), pl.BlockSpec((tm, LANES), row),
            pl.BlockSpec((tm, LANES), row), pl.BlockSpec((tm, LANES), row),
            pl.BlockSpec((1, LANES), const), pl.BlockSpec((1, LANES), const),
            pl.BlockSpec((1, LANES), const), pl.BlockSpec((1, LANES), const),
        ],
        out_specs=pl.BlockSpec((tm, PROJ_TN), lambda i, j: (i, j)),
        scratch_shapes=[pltpu.VMEM((tm, d), BF16)],
        compiler_params=_cparams("parallel", "arbitrary"),
        name="proj",
    )(x, gain.reshape(1, d), w_packed, c64, s64, c128, s128, *gains)


def _merge_kernel(lam_init, x_ref, oa_ref, od_ref, ga_ref, gb_ref, g_ref, wa_ref, wb_ref, wo_ref,
                  o_ref, a_scr):
    for h in range(DA_HEADS):
        blk = oa_ref[:, h * DA_V_DIM:(h + 1) * DA_V_DIM]
        ms = jnp.mean(blk * blk, axis=-1, keepdims=True)
        y = ((blk * lax.rsqrt(ms + EPS)) * g_ref[...]) * (1.0 - lam_init)
        a_scr[:, h * DA_V_DIM:(h + 1) * DA_V_DIM] = y.astype(BF16)
    a = jnp.dot(a_scr[...], wa_ref[...], preferred_element_type=F32)
    b = jnp.dot(od_ref[...].astype(BF16), wb_ref[...], preferred_element_type=F32)
    merged = ga_ref[...] * a + gb_ref[...] * b
    o_ref[...] = x_ref[...] + jnp.dot(merged.astype(BF16), wo_ref[...], preferred_element_type=F32)


def _merge(x, oa, od, z, gain, wa, wb, wo, lam_init, tm):
    m, d = x.shape
    row = lambda i: (i, 0)
    const = lambda i: (0, 0)
    ga_blk = _OFF["ga"][0] // d
    gb_blk = _OFF["gb"][0] // d
    return pl.pallas_call(
        functools.partial(_merge_kernel, lam_init),
        out_shape=jax.ShapeDtypeStruct((m, d), F32),
        grid=(m // tm,),
        in_specs=[
            pl.BlockSpec((tm, d), row), pl.BlockSpec((tm, d), row), pl.BlockSpec((tm, d), row),
            pl.BlockSpec((tm, d), lambda i: (i, ga_blk)), pl.BlockSpec((tm, d), lambda i: (i, gb_blk)),
            pl.BlockSpec((1, DA_V_DIM), const),
            pl.BlockSpec((d, d), const), pl.BlockSpec((d, d), const), pl.BlockSpec((d, d), const),
        ],
        out_specs=pl.BlockSpec((tm, d), row),
        scratch_shapes=[pltpu.VMEM((tm, d), BF16)],
        compiler_params=_cparams("parallel"),
        name="merge",
    )(x, oa, od, z, z, gain.reshape(1, DA_V_DIM), wa, wb, wo)


def _lambda_of(lp, lam_init):
    a = jnp.sum(lp[0:1, :] * lp[1:2, :], axis=-1, keepdims=True)
    b = jnp.sum(lp[2:3, :] * lp[3:4, :], axis=-1, keepdims=True)
    return jnp.exp(a) - jnp.exp(b) + lam_init


def _softmax_step(s, v, m_scr, l_scr, acc_scr):
    m_prev = m_scr[...]
    m_new = jnp.maximum(m_prev, jnp.max(s, axis=-1, keepdims=True))
    alpha = jnp.exp(m_prev - m_new)
    p = jnp.exp(s - m_new)
    l_scr[...] = alpha * l_scr[...] + jnp.sum(p, axis=-1, keepdims=True)
    acc_scr[...] = alpha * acc_scr[...] + jnp.dot(p.astype(BF16), v, preferred_element_type=F32)
    m_scr[...] = m_new


def _flash_step(s, v_ext, m_ref, acc_ref):
    m_prev = m_ref[...]
    m_new = jnp.maximum(m_prev, jnp.max(s, axis=-1, keepdims=True))
    alpha = jnp.exp2(m_prev - m_new)
    p = jnp.exp2(s - jnp.concatenate([m_new] * (s.shape[1] // LANES), axis=1))
    acc_ref[...] = (jnp.concatenate([alpha] * (acc_ref.shape[1] // LANES), axis=1) * acc_ref[...]
                    + jnp.dot(p.astype(BF16), v_ext, preferred_element_type=F32))
    m_ref[...] = m_new


def _diff_attn_kernel(lam_init, q_ref, kt_ref, v_ref, lp_ref, o_ref, lhs_scr, m_scr, acc_scr):
    qt = pl.program_id(1)
    q = q_ref[...]
    lane = lax.broadcasted_iota(I32, (TQ, LANES), 1)
    for g in range(2):
        qg = q[:, g * LANES:(g + 1) * LANES]
        for mp in range(2):
            keep = (lane < DA_QK_DIM) if mp == 0 else (lane >= DA_QK_DIM)
            r = (g * 2 + mp) * TQ
            lhs_scr[r:r + TQ, :] = jnp.where(keep, qg, jnp.zeros_like(qg))
    m_scr[...] = jnp.full(m_scr.shape, NEG, F32)
    acc_scr[...] = jnp.zeros(acc_scr.shape, F32)
    rows = 4 * TQ

    def step(kt, masked):
        s = jnp.dot(lhs_scr[...], kt_ref[0, kt], preferred_element_type=F32)
        if masked:
            qpos = qt * TQ + (lax.broadcasted_iota(I32, (rows, TK), 0) & (TQ - 1))
            col = lax.broadcasted_iota(I32, (rows, TK), 1) + kt * TK
            s = jnp.where(col <= qpos, s, NEG)
        v = v_ref[pl.ds(pl.multiple_of(kt * TK, TK), TK), :]
        _flash_step(s, v, m_scr, acc_scr)

    def body(kt, carry):
        step(kt, False)
        return carry

    lax.fori_loop(0, qt, body, 0)
    step(qt, True)
    acc = acc_scr[...]
    o = acc[:, :DA_V_DIM] / acc[:, DA_V_DIM:]
    lam = _lambda_of(lp_ref[...], lam_init)
    for g in range(2):
        r = g * 2 * TQ
        o_ref[:, g * DA_V_DIM:(g + 1) * DA_V_DIM] = o[r:r + TQ, :] - lam * o[r + TQ:r + 2 * TQ, :]


def _diff_attn(q, kt, v_ext, lp, lam_init):
    t = q.shape[0]
    n_kt = t // TK
    assert TQ == TK
    return pl.pallas_call(
        functools.partial(_diff_attn_kernel, lam_init),
        out_shape=jax.ShapeDtypeStruct((t, DA_HEADS * DA_V_DIM), F32),
        grid=(DA_KV_HEADS, t // TQ),
        in_specs=[
            pl.BlockSpec((TQ, 2 * LANES), lambda kv, i: (i, kv)),
            pl.BlockSpec((1, n_kt, LANES, TK), lambda kv, i: (kv, 0, 0, 0)),
            pl.BlockSpec((t, 2 * DA_V_DIM), lambda kv, i: (0, kv)),
            pl.BlockSpec((4, DA_QK_DIM), lambda kv, i: (0, 0)),
        ],
        out_specs=pl.BlockSpec((TQ, 2 * DA_V_DIM), lambda kv, i: (i, kv)),
        scratch_shapes=[pltpu.VMEM((4 * TQ, LANES), BF16), pltpu.VMEM((4 * TQ, LANES), F32),
                        pltpu.VMEM((4 * TQ, 2 * DA_V_DIM), F32)],
        compiler_params=_cparams("parallel", "arbitrary"),
        name="diff_attn",
    )(q, kt, v_ext, lp)


def _float_key(s):
    s = jnp.where(s == 0.0, 0.0, s)
    bits = pltpu.bitcast(s, I32)
    return jnp.where(bits < 0, bits ^ jnp.int32(0x7FFFFFFF), bits)


def _key_float(key):
    bits = jnp.where(key < 0, key ^ jnp.int32(0x7FFFFFFF), key)
    return pltpu.bitcast(bits, F32)


KEY_LOWEST_FINITE = INT_MIN + 0x00800000


def _kth_largest_key(count_ge, rows, k):
    def bit_body(i, t):
        cand = t + lax.shift_left(jnp.int32(1), 31 - i)
        return jnp.where(count_ge(cand) >= k, cand, t)

    return lax.fori_loop(0, 32, bit_body, jnp.full((rows, 1), INT_MIN, I32))


def _select_kernel(topk, t_real, qi_ref, wi_ref, ki_ref, o_ref, lhs_scr, sc_scr, short_scr, t_scr, n_scr):
    qt = pl.program_id(0)
    n_kt = sc_scr.shape[0]
    n_rows = jnp.where(qt * TQ_SEL < t_real, jnp.minimum((qt + 1) * TQ_SEL, t_real), 1)
    n_k = (n_rows + TK - 1) // TK
    lane = lax.broadcasted_iota(I32, (TQ_SEL, LANES), 1)
    for h in range(IDX_HEADS):
        pair = qi_ref[:, (h // 2) * LANES:(h // 2 + 1) * LANES]
        keep = (lane < IDX_DIM) if h % 2 == 0 else (lane >= IDX_DIM)
        lhs_scr[h * TQ_SEL:(h + 1) * TQ_SEL, :] = jnp.where(keep, pair, jnp.zeros_like(pair))
    wi = wi_ref[...]
    w_cols = [wi[:, h:h + 1] for h in range(IDX_HEADS)]
    qpos = qt * TQ_SEL + lax.broadcasted_iota(I32, (TQ_SEL, TK), 0)
    qpos = jnp.where(qpos < t_real, qpos, 0)
    col = lax.broadcasted_iota(I32, (TQ_SEL, TK), 1)

    def score_body(kt, carry):
        d = jnp.dot(lhs_scr[...], ki_ref[kt], preferred_element_type=F32)
        sc = jnp.zeros((TQ_SEL, TK), F32)
        for h in range(IDX_HEADS):
            sc = sc + jnp.maximum(d[h * TQ_SEL:(h + 1) * TQ_SEL, :], 0.0) * w_cols[h]
        sc = jnp.where(sc == 0.0, 0.0, sc)
        sc_scr[kt] = jnp.where(col + kt * TK <= qpos, sc, -jnp.inf)
        return carry

    lax.fori_loop(0, n_k, score_body, 0)

    def count(pred):
        def body(kt, acc):
            hit = jnp.where(pred(sc_scr[kt], col + kt * TK), 1.0, 0.0)
            for c in range(TK // LANES):
                acc = acc + hit[:, c * LANES:(c + 1) * LANES]
            return acc

        acc = lax.fori_loop(0, n_k, body, jnp.zeros((TQ_SEL, LANES), F32))
        return jnp.sum(acc, axis=-1, keepdims=True)

    def shortlist_body(rg, carry):
        rows8 = pl.ds(pl.multiple_of(rg * 8, 8), 8)

        def insert_body(kt, tops):
            for c in range(TK // LANES):
                x = sc_scr[kt, rows8, c * LANES:(c + 1) * LANES]
                new = []
                for top in tops:
                    new.append(jnp.maximum(top, x))
                    x = jnp.minimum(top, x)
                tops = tuple(new)
            return tops

        tops = lax.fori_loop(0, n_k, insert_body, (jnp.full((8, LANES), -jnp.inf, F32),) * SHORTLIST)
        for j in range(SHORTLIST):
            short_scr[j, rows8, :] = _float_key(tops[j])
        return carry

    lax.fori_loop(0, TQ_SEL // 8, shortlist_body, 0)

    def count_short(cand):
        acc = jnp.zeros((TQ_SEL, LANES), F32)
        for j in range(SHORTLIST):
            acc = acc + jnp.where(short_scr[j] >= cand, 1.0, 0.0)
        return jnp.sum(acc, axis=-1, keepdims=True)

    kf = float(topk)
    t_key = jnp.maximum(_kth_largest_key(count_short, TQ_SEL, kf), KEY_LOWEST_FINITE)
    n_ge = count(lambda s, idx: s >= _key_float(t_key))
    t_scr[...] = t_key
    n_scr[...] = n_ge

    @pl.when(jnp.max(jnp.abs(n_ge - count_short(t_key))) > 0.0)
    def _():
        t_full = _kth_largest_key(lambda cand: count(lambda s, idx: s >= _key_float(cand)), TQ_SEL, kf)
        t_full = jnp.maximum(t_full, KEY_LOWEST_FINITE)
        t_scr[...] = t_full
        n_scr[...] = count(lambda s, idx: s >= _key_float(t_full))

    t = _key_float(t_scr[...])
    n_ge = n_scr[...]

    @pl.when(jnp.max(n_ge) > kf)
    def _():
        need = kf - count(lambda s, idx: s > t)

        def bit_body(i, x):
            cand = x + lax.shift_left(jnp.int32(1), 14 - i)
            below = count(lambda s, idx: jnp.logical_and(s == t, idx < cand))
            return jnp.where(below < need, cand, x)

        last = lax.fori_loop(0, 15, bit_body, jnp.zeros((TQ_SEL, 1), I32))

        def drop_body(kt, carry):
            s = sc_scr[kt]
            drop = jnp.logical_and(s == t, col + kt * TK > last)
            sc_scr[kt] = jnp.where(drop, -jnp.inf, s)
            return carry

        lax.fori_loop(0, n_k, drop_body, 0)

    def write_body(kt, carry):
        o_ref[kt] = jnp.where(sc_scr[kt] >= t, 0.0, NEG).astype(BF16)
        return carry

    lax.fori_loop(0, n_k, write_body, 0)

    def fill_body(kt, carry):
        o_ref[kt] = jnp.full((TQ_SEL, TK), NEG, BF16)
        return carry

    lax.fori_loop(n_k, n_kt, fill_body, 0)


def _select(qi, wi, ki2, topk, t_real):
    t = qi.shape[0]
    n_kt = t // TK
    assert t < 2 ** 15
    return pl.pallas_call(
        functools.partial(_select_kernel, topk, t_real),
        out_shape=jax.ShapeDtypeStruct((n_kt, t, TK), BF16),
        grid=(t // TQ_SEL,),
        in_specs=[
            pl.BlockSpec((TQ_SEL, IDX_HEADS * IDX_DIM), lambda i: (i, 0)),
            pl.BlockSpec((TQ_SEL, LANES), lambda i: (i, 0)),
            pl.BlockSpec((n_kt, LANES, TK), lambda i: (0, 0, 0)),
        ],
        out_specs=pl.BlockSpec((n_kt, TQ_SEL, TK), lambda i: (0, i, 0)),
        scratch_shapes=[pltpu.VMEM((IDX_HEADS * TQ_SEL, LANES), BF16),
                        pltpu.VMEM((n_kt, TQ_SEL, TK), F32),
                        pltpu.VMEM((SHORTLIST, TQ_SEL, LANES), I32),
                        pltpu.VMEM((TQ_SEL, 1), I32), pltpu.VMEM((TQ_SEL, 1), F32)],
        compiler_params=_cparams("parallel"),
        name="dsa_select",
    )(qi, wi, ki2)


def _dsa_attn_kernel(qt_ref, kt_ref_s, q_ref, kt_ref, v_ref, b_ref, o_ref, lhs_scr, m_scr, acc_scr):
    step = pl.program_id(0)
    qt, kt = qt_ref[step], kt_ref_s[step]

    @pl.when(kt == 0)
    def _():
        for kv in range(DSA_KV_HEADS):
            for g in range(2):
                h = kv * 2 + g
                lhs_scr[kv, g * TQ:(g + 1) * TQ, :] = q_ref[:, h * LANES:(h + 1) * LANES]
        m_scr[...] = jnp.full(m_scr.shape, NEG, F32)
        acc_scr[...] = jnp.zeros(acc_scr.shape, F32)

    bias = b_ref[0].astype(F32)
    bias2 = jnp.concatenate([bias, bias], axis=0)
    for kv in range(DSA_KV_HEADS):
        s = jnp.dot(lhs_scr[kv], kt_ref[kv, 0], preferred_element_type=F32) + bias2
        v = v_ref[:, kv * 2 * DSA_HEAD_DIM:(kv + 1) * 2 * DSA_HEAD_DIM]
        _flash_step(s, v, m_scr.at[kv], acc_scr.at[kv])

    @pl.when(kt == qt)
    def _():
        for kv in range(DSA_KV_HEADS):
            acc = acc_scr[kv]
            o = acc[:, :DSA_HEAD_DIM] / acc[:, DSA_HEAD_DIM:]
            for g in range(2):
                h = kv * 2 + g
                o_ref[:, h * DSA_HEAD_DIM:(h + 1) * DSA_HEAD_DIM] = o[g * TQ:(g + 1) * TQ, :]


def _dsa_attn(q, kt, v_ext, bias):
    t = q.shape[0]
    assert TQ == TK
    kv, d = DSA_KV_HEADS, DSA_HEAD_DIM
    n_t = t // TQ
    qt_of_step = np.concatenate([np.full(i + 1, i, np.int32) for i in range(n_t)])
    kt_of_step = np.concatenate([np.arange(i + 1, dtype=np.int32) for i in range(n_t)])
    grid_spec = pltpu.PrefetchScalarGridSpec(
        num_scalar_prefetch=2,
        grid=(qt_of_step.shape[0],),
        in_specs=[
            pl.BlockSpec((TQ, DSA_HEADS * d), lambda s, qt, kt: (qt[s], 0)),
            pl.BlockSpec((kv, 1, LANES, TK), lambda s, qt, kt: (0, kt[s], 0, 0)),
            pl.BlockSpec((TK, kv * 2 * d), lambda s, qt, kt: (kt[s], 0)),
            pl.BlockSpec((1, TQ, TK), lambda s, qt, kt: (kt[s], qt[s], 0)),
        ],
        out_specs=pl.BlockSpec((TQ, DSA_HEADS * d), lambda s, qt, kt: (qt[s], 0)),
        scratch_shapes=[pltpu.VMEM((kv, 2 * TQ, LANES), BF16), pltpu.VMEM((kv, 2 * TQ, LANES), F32),
                        pltpu.VMEM((kv, 2 * TQ, 2 * d), F32)],
    )
    return pl.pallas_call(
        _dsa_attn_kernel,
        out_shape=jax.ShapeDtypeStruct((t, DSA_HEADS * d), F32),
        grid_spec=grid_spec,
        compiler_params=_cparams("arbitrary"),
        name="dsa_attn",
    )(jnp.asarray(qt_of_step), jnp.asarray(kt_of_step), q, kt, v_ext, bias)


STREAM_ROWS = 16


def _paged_attn_kernel(mode, lam_init, n_pages, *refs):
    npg = PAGES_PER_STEP
    pt_ref, wq_ref = refs[0], refs[1]
    k_refs = refs[2:2 + npg]
    v_refs = refs[2 + npg:2 + 2 * npg]
    knew_ref, vnew_ref = refs[2 + 2 * npg], refs[3 + 2 * npg]
    pos = 4 + 2 * npg
    if mode == "da":
        lp_ref = refs[pos]
        pos += 1
    else:
        bias_ref, bnew_ref = refs[pos], refs[pos + 1]
        pos += 2
    o_ref, m_scr, l_scr, acc_scr = refs[pos:pos + 4]
    j = pl.program_id(1)
    wq = wq_ref[0]

    @pl.when(j == 0)
    def _():
        s_new = jnp.sum(wq.astype(F32) * knew_ref[0], axis=-1, keepdims=True)
        if mode == "dsa":
            s_new = s_new + bnew_ref[0][:, 0:1].astype(F32)
        m_scr[...] = s_new
        l_scr[...] = jnp.ones(l_scr.shape, F32)
        acc_scr[...] = jnp.broadcast_to(vnew_ref[0], acc_scr.shape)

    def token_rows(ref):
        n_kv = ref.shape[2] // PAGE_SIZE
        return jnp.concatenate([ref[0, 0, pl.ds(kv, PAGE_SIZE, stride=n_kv), :] for kv in range(n_kv)], axis=1)

    v = jnp.concatenate([token_rows(r) for r in v_refs], axis=0).astype(BF16)
    if mode == "da":
        kt = jnp.concatenate([r[0, 0].reshape(-1, PAGE_SIZE) for r in k_refs], axis=1).astype(BF16)
        s = jnp.dot(wq, kt, preferred_element_type=F32)
    else:
        k = jnp.concatenate([token_rows(r) for r in k_refs], axis=0).astype(BF16)
        s = lax.dot_general(wq, k, (((1,), (1,)), ((), ())), preferred_element_type=F32)
        s = s + bias_ref[0].astype(F32)
    _softmax_step(s, v, m_scr, l_scr, acc_scr)

    @pl.when(j == n_pages // npg - 1)
    def _():
        o = acc_scr[...] / l_scr[...]
        if mode == "da":
            lam = _lambda_of(lp_ref[...], lam_init)
            for kv in range(DA_KV_HEADS):
                for g in range(2):
                    r = kv * 4 + g * 2
                    blk = slice(kv * DA_V_DIM, (kv + 1) * DA_V_DIM)
                    h = kv * 2 + g
                    o_ref[0, :, h * DA_V_DIM:(h + 1) * DA_V_DIM] = o[r:r + 1, blk] - lam * o[r + 1:r + 2, blk]
        else:
            for kv in range(DSA_KV_HEADS):
                for g in range(2):
                    r = kv * 2 + g
                    o_ref[0, :, r * DSA_HEAD_DIM:(r + 1) * DSA_HEAD_DIM] = (
                        o[r:r + 1, kv * DSA_HEAD_DIM:(kv + 1) * DSA_HEAD_DIM])


def _paged_attn(mode, layer, page_table, wq, k_cache, v_cache, k_new, v_new, extra, lam_init):
    nb, n_pages = page_table.shape
    npg = PAGES_PER_STEP
    assert n_pages % npg == 0
    width = k_new.shape[2]

    def page_spec(cache, i):
        tail = cache.shape[2:]
        zeros = (0,) * len(tail)
        return pl.BlockSpec((1, 1) + tail, lambda b, j, pt: (layer, pt[b * n_pages + j * npg + i]) + zeros)

    per_seq = lambda b, j, pt: (b, 0, 0)
    in_specs = [pl.BlockSpec((1, STREAM_ROWS, width), per_seq)]
    in_specs += [page_spec(k_cache, i) for i in range(npg)] + [page_spec(v_cache, i) for i in range(npg)]
    in_specs += [pl.BlockSpec((1, 1, width), per_seq), pl.BlockSpec((1, 1, width), per_seq)]
    args = [wq] + [k_cache] * npg + [v_cache] * npg + [k_new, v_new]
    if mode == "da":
        in_specs.append(pl.BlockSpec((4, DA_QK_DIM), lambda b, j, pt: (0, 0)))
        args.append(extra)
    else:
        in_specs.append(pl.BlockSpec((1, 1, npg * PAGE_SIZE), lambda b, j, pt: (b, 0, j)))
        in_specs.append(pl.BlockSpec((1, 1, PAGE_SIZE), lambda b, j, pt: (b, 0, n_pages)))
        args += [extra, extra]
    grid_spec = pltpu.PrefetchScalarGridSpec(
        num_scalar_prefetch=1,
        grid=(nb, n_pages // npg),
        in_specs=in_specs,
        out_specs=pl.BlockSpec((1, 1, 8 * LANES), per_seq),
        scratch_shapes=[pltpu.VMEM((STREAM_ROWS, 1), F32), pltpu.VMEM((STREAM_ROWS, 1), F32),
                        pltpu.VMEM((STREAM_ROWS, width), F32)],
    )
    return pl.pallas_call(
        functools.partial(_paged_attn_kernel, mode, lam_init, n_pages),
        out_shape=jax.ShapeDtypeStruct((nb, 1, 8 * LANES), F32),
        grid_spec=grid_spec,
        compiler_params=_cparams("parallel", "arbitrary"),
        name="paged_" + mode,
    )(page_table.reshape(-1), *args)


def _paged_index_kernel(*refs):
    npg = len(refs) - 4
    q_ref, w_ref = refs[1], refs[2]
    k_refs = refs[3:3 + npg]
    o_ref = refs[3 + npg]
    kt = jnp.concatenate([r[0, 0] for r in k_refs], axis=1).astype(BF16)
    d = jnp.dot(q_ref[0], kt, preferred_element_type=F32)
    o_ref[0] = jnp.sum(jnp.maximum(d, 0.0) * w_ref[0], axis=0, keepdims=True)


def _paged_index(layer, page_table, qi, wi, k_cache):
    nb, n_pages = page_table.shape
    npg = math.gcd(INDEX_PAGES_PER_STEP, n_pages)

    def page_spec(i):
        return pl.BlockSpec((1, 1, IDX_DIM, PAGE_SIZE),
                            lambda b, j, pt: (layer, pt[b * n_pages + j * npg + i], 0, 0))

    per_seq = lambda b, j, pt: (b, 0, 0)
    grid_spec = pltpu.PrefetchScalarGridSpec(
        num_scalar_prefetch=1,
        grid=(nb, n_pages // npg),
        in_specs=[pl.BlockSpec((1, IDX_HEADS, IDX_DIM), per_seq), pl.BlockSpec((1, IDX_HEADS, 1), per_seq)]
        + [page_spec(i) for i in range(npg)],
        out_specs=pl.BlockSpec((1, 1, npg * PAGE_SIZE), lambda b, j, pt: (b, 0, j)),
    )
    return pl.pallas_call(
        _paged_index_kernel,
        out_shape=jax.ShapeDtypeStruct((nb, 1, n_pages * PAGE_SIZE), F32),
        grid_spec=grid_spec,
        compiler_params=_cparams("parallel", "arbitrary"),
        name="paged_index",
    )(page_table.reshape(-1), qi, wi, *([k_cache] * npg))


def _sample_select_kernel(topk, sc_ref, qi_ref, kin_ref, wi_ref, o_ref, key_scr):
    nb, past = sc_ref.shape
    width = past + LANES
    prod = qi_ref[...] * kin_ref[...]
    lane = lax.broadcasted_iota(I32, (nb, LANES), 1)
    wi = wi_ref[...]
    s_new = jnp.zeros((nb, 1), F32)
    for h in range(IDX_HEADS):
        blk = prod[:, (h // 2) * LANES:(h // 2 + 1) * LANES]
        keep = (lane < IDX_DIM) if h % 2 == 0 else (lane >= IDX_DIM)
        d = jnp.sum(jnp.where(keep, blk, 0.0), axis=-1, keepdims=True)
        s_new = s_new + jnp.maximum(d, 0.0) * wi[:, h:h + 1]
    key_scr[:, 0:past] = _float_key(sc_ref[...])
    key_scr[:, past:width] = jnp.where(lane == 0, _float_key(s_new), INT_MIN)
    col = lax.broadcasted_iota(I32, (nb, width), 1)

    def count(pred):
        return jnp.sum(jnp.where(pred(key_scr[...], col), 1.0, 0.0), axis=-1, keepdims=True)

    kf = float(topk)
    t = _kth_largest_key(lambda cand: count(lambda keys, idx: keys >= cand), nb, kf)
    t = jnp.maximum(t, INT_MIN + 1)
    n_ge = count(lambda keys, idx: keys >= t)

    @pl.when(jnp.max(n_ge) > kf)
    def _():
        need = kf - count(lambda keys, idx: keys > t)

        def bit_body(i, x):
            cand = x + lax.shift_left(jnp.int32(1), 14 - i)
            below = count(lambda keys, idx: jnp.logical_and(keys == t, idx < cand))
            return jnp.where(below < need, cand, x)

        last = lax.fori_loop(0, 15, bit_body, jnp.zeros((nb, 1), I32))
        keys = key_scr[...]
        key_scr[...] = jnp.where(jnp.logical_and(keys == t, col > last), INT_MIN, keys)

    o_ref[...] = jnp.where(key_scr[...] >= t, 0.0, NEG).astype(BF16)


def _sample_select(scores, qi, ki_new8, wi, topk):
    nb, past = scores.shape
    assert past + LANES < 2 ** 15
    return pl.pallas_call(
        functools.partial(_sample_select_kernel, topk),
        out_shape=jax.ShapeDtypeStruct((nb, past + LANES), BF16),
        scratch_shapes=[pltpu.VMEM((nb, past + LANES), I32)],
        compiler_params=pltpu.CompilerParams(vmem_limit_bytes=VMEM_LIMIT_BYTES),
        name="sample_select",
    )(scores, qi, ki_new8, wi)


def _rope_tables(pos, d):
    r = d // 4
    half = r // 2
    inv = 1.0 / (ROPE_THETA ** (jnp.arange(half, dtype=F32) * (2.0 / r)))
    ang = pos.astype(F32)[:, None] * inv[None, :]
    cos, sin = jnp.cos(ang), jnp.sin(ang)
    n = pos.shape[0]
    c = jnp.concatenate([cos, cos, jnp.ones((n, d - r), F32)], axis=1)
    s = jnp.concatenate([-sin, sin, jnp.zeros((n, d - r), F32)], axis=1)
    reps = LANES // d
    return jnp.tile(c, (1, reps)), jnp.tile(s, (1, reps))


def _pack_w_in(w):
    parts = jnp.split(w, np.cumsum(SPLITS).tolist(), axis=1)
    qa, ka, va, qd, kd, vd, qi, ki, wi, gates = parts
    ga, gb = jnp.split(gates, 2, axis=1)
    pad = jnp.zeros((w.shape[0], PROJ_TN - IDX_DIM - IDX_HEADS), w.dtype)
    return jnp.concatenate([qa, ka, va, qd, kd, vd, ga, gb, qi, ki, wi, pad], axis=1).astype(BF16)


def _cols(z, name):
    off, width = _OFF[name]
    return z[:, off:off + width]


def _key_tiles(k, n_heads):
    t = k.shape[0]
    kt = k.astype(BF16).reshape(t // TK, TK, n_heads, LANES)
    return jnp.transpose(kt, (2, 0, 3, 1))


def _with_ones(v, n_heads):
    t = v.shape[0]
    v3 = v.astype(BF16).reshape(t, n_heads, LANES)
    return jnp.concatenate([v3, jnp.ones_like(v3)], axis=2).reshape(t, n_heads * 2 * LANES)


def _layer_weights(l, p):
    return dict(
        norm_ffn1=p["norm_ffn1"][l], w_gu1=p["ffn1_w_gu"][l].astype(BF16), w_d1=p["ffn1_w_down"][l].astype(BF16),
        norm_mix=p["norm_mix"][l], w_in=_pack_w_in(p["w_in"][l]),
        gains=tuple(jnp.tile(p[n][l], LANES // p[n].shape[1]).reshape(1, LANES)
                    for n in ("da_q_norm", "da_k_norm", "dsa_q_norm", "dsa_k_norm")),
        lp=p["da_lambda"][l], out_norm=p["da_out_norm"][l],
        wa=p["w_branch_a"][l].astype(BF16), wb=p["w_branch_b"][l].astype(BF16), wo=p["w_out"][l].astype(BF16),
        norm_ffn2=p["norm_ffn2"][l], w_gu2=p["ffn2_w_gu"][l].astype(BF16), w_d2=p["ffn2_w_down"][l].astype(BF16),
    )


def _prompt_layer(x, w, tabs, lam_init, topk, t_real):
    x = _ffn(x, w["norm_ffn1"], w["w_gu1"], w["w_d1"], TM_FFN)
    z = _proj(x, w["norm_mix"], w["w_in"], tabs, w["gains"], TM_PROJ)
    ka, va, kd, vd = _cols(z, "ka"), _cols(z, "va"), _cols(z, "kd"), _cols(z, "vd")
    kw = _cols(z, "kw")
    ki = kw[:, :IDX_DIM]
    qa = (_cols(z, "qa") * (DA_QK_DIM ** -0.5 * LOG2E)).astype(BF16)
    oa = _diff_attn(qa, _key_tiles(ka, DA_KV_HEADS), _with_ones(va, DA_KV_HEADS), w["lp"], lam_init)
    ki2 = _key_tiles(jnp.concatenate([ki, ki], axis=1), 1)[0]
    wi = kw[:, IDX_DIM:IDX_DIM + LANES]
    bias = _select(_cols(z, "qi").astype(BF16), wi, ki2, topk, t_real)
    qd = (_cols(z, "qd") * (DSA_HEAD_DIM ** -0.5 * LOG2E)).astype(BF16)
    od = _dsa_attn(qd, _key_tiles(kd, DSA_KV_HEADS), _with_ones(vd, DSA_KV_HEADS), bias)
    x = _merge(x, oa, od, z, w["out_norm"], w["wa"], w["wb"], w["wo"], lam_init, TM_MERGE)
    x = _ffn(x, w["norm_ffn2"], w["w_gu2"], w["w_d2"], TM_FFN)
    return x, (ka, va, kd, vd, ki)


def _expand_queries(q, n_kv, per_kv_rows, width, scale):
    nb = q.shape[0]
    eye = jnp.eye(n_kv, dtype=q.dtype)
    out = jnp.einsum("bkrw,kc->bkrcw", q * scale, eye).reshape(nb, n_kv * per_kv_rows, n_kv * width)
    pad = STREAM_ROWS - n_kv * per_kv_rows
    return jnp.pad(out, ((0, 0), (0, pad), (0, 0))).astype(BF16)


def _cache_views(cache_da_k, cache_da_v, cache_dsa_k, cache_dsa_v, cache_idx_k):
    depth, n_pool = cache_da_k.shape[:2]
    rows = lambda c: c.reshape(depth, n_pool, -1, c.shape[-1])
    return (jnp.transpose(cache_da_k, (0, 1, 3, 4, 5, 2)), rows(cache_da_v), rows(cache_dsa_k),
            rows(cache_dsa_v), jnp.transpose(cache_idx_k, (0, 1, 3, 2)))


def _sample_layer(x, w, tabs, lam_init, topk, layer, caches, page_table):
    da_kt, da_v, dsa_k, dsa_v, idx_kt = caches
    nb = page_table.shape[0]
    rows = x.shape[0]
    x = _ffn(x, w["norm_ffn1"], w["w_gu1"], w["w_d1"], rows)
    z = _proj(x, w["norm_mix"], w["w_in"], tabs, w["gains"], rows)
    ka, va, kd, vd = (_cols(z, n)[:nb] for n in ("ka", "va", "kd", "vd"))
    kw = _cols(z, "kw")[:nb]
    ki = kw[:, :IDX_DIM]
    wi = kw[:, IDX_DIM:IDX_DIM + IDX_HEADS]
    qa = _cols(z, "qa")[:nb].reshape(nb, DA_KV_HEADS, 2, 2, DA_QK_DIM)
    blk = jnp.einsum("bkgmd,mn->bkgmnd", qa * DA_QK_DIM ** -0.5, jnp.eye(2, dtype=F32))
    wq_da = _expand_queries(blk.reshape(nb, DA_KV_HEADS, 4, 2 * DA_QK_DIM), DA_KV_HEADS, 4, LANES, 1.0)
    oa = _paged_attn("da", layer, page_table, wq_da, da_kt, da_v,
                     ka.reshape(nb, 1, -1), va.reshape(nb, 1, -1), w["lp"], lam_init)
    qi = _cols(z, "qi")[:nb]
    scores = _paged_index(layer, page_table, qi.reshape(nb, IDX_HEADS, IDX_DIM).astype(BF16),
                          wi.reshape(nb, IDX_HEADS, 1), idx_kt)
    bias = _sample_select(scores.reshape(nb, -1), qi, jnp.tile(ki, (1, IDX_HEADS)),
                          jnp.pad(wi, ((0, 0), (0, LANES - IDX_HEADS))), topk)
    qd = _cols(z, "qd")[:nb].reshape(nb, DSA_KV_HEADS, 2, DSA_HEAD_DIM)
    wq_dsa = _expand_queries(qd, DSA_KV_HEADS, 2, DSA_HEAD_DIM, DSA_HEAD_DIM ** -0.5)
    od = _paged_attn("dsa", layer, page_table, wq_dsa, dsa_k, dsa_v,
                     kd.reshape(nb, 1, -1), vd.reshape(nb, 1, -1), bias.reshape(nb, 1, -1), None)
    pad = ((0, rows - nb), (0, 0))
    x = _merge(x, jnp.pad(oa.reshape(nb, -1), pad), jnp.pad(od.reshape(nb, -1), pad), z,
               w["out_norm"], w["wa"], w["wb"], w["wo"], lam_init, rows)
    x = _ffn(x, w["norm_ffn2"], w["w_gu2"], w["w_d2"], rows)
    return x, (ka, va, kd, vd, ki)


def kernel(x_prompt, x_sample, cache_da_k, cache_da_v, cache_dsa_k, cache_dsa_v, cache_idx_k, page_table,
           meta_tokens, norm_ffn1, ffn1_w_gu, ffn1_w_down, norm_mix, w_in, da_q_norm, da_k_norm,
           dsa_q_norm, dsa_k_norm, da_lambda, da_out_norm, w_branch_a, w_branch_b, w_out,
           norm_ffn2, ffn2_w_gu, ffn2_w_down):
    params = dict(norm_ffn1=norm_ffn1, ffn1_w_gu=ffn1_w_gu, ffn1_w_down=ffn1_w_down, norm_mix=norm_mix,
                  w_in=w_in, da_q_norm=da_q_norm, da_k_norm=da_k_norm, dsa_q_norm=dsa_q_norm,
                  dsa_k_norm=dsa_k_norm, da_lambda=da_lambda, da_out_norm=da_out_norm,
                  w_branch_a=w_branch_a, w_branch_b=w_branch_b, w_out=w_out, norm_ffn2=norm_ffn2,
                  ffn2_w_gu=ffn2_w_gu, ffn2_w_down=ffn2_w_down)
    depth, d_model = norm_mix.shape
    assert w_in.shape[2] == sum(SPLITS) + 2 * d_model and d_model == DA_HEADS * DA_V_DIM
    assert x_prompt.shape[0] == 1 and x_sample.shape[1] == 1
    weights = [_layer_weights(l, params) for l in range(depth)]
    lam_inits = [0.8 - 0.6 * math.exp(-0.3 * l) for l in range(depth)]

    seq = x_prompt.shape[1]
    t = seq + N_META
    t_pad = -(-t // T_ALIGN) * T_ALIGN
    xp = jnp.concatenate([meta_tokens.astype(F32), x_prompt[0], jnp.zeros((t_pad - t, d_model), F32)], axis=0)
    pos_p = jnp.arange(t_pad, dtype=jnp.int32)
    tabs_p = _rope_tables(pos_p, DA_QK_DIM) + _rope_tables(pos_p, DSA_HEAD_DIM)
    topk_p = min(TOPK_MAX, t // 4)
    rows_p = []
    for l in range(depth):
        xp, rows = _prompt_layer(xp, weights[l], tabs_p, lam_inits[l], topk_p, t)
        rows_p.append(tuple(r[:t] for r in rows))
    y_prompt = xp[N_META:t][None]

    nb, n_pages = page_table.shape
    past = n_pages * PAGE_SIZE
    rows_s_pad = -(-nb // 16) * 16
    xs = jnp.pad(x_sample[:, 0, :], ((0, rows_s_pad - nb), (0, 0)))
    pos_s = jnp.full((rows_s_pad,), past, jnp.int32)
    tabs_s = _rope_tables(pos_s, DA_QK_DIM) + _rope_tables(pos_s, DSA_HEAD_DIM)
    topk_s = min(TOPK_MAX, (past + 1) // 4)
    rows_s = []
    caches = _cache_views(cache_da_k, cache_da_v, cache_dsa_k, cache_dsa_v, cache_idx_k)
    for l in range(depth):
        xs, rows = _sample_layer(xs, weights[l], tabs_s, lam_inits[l], topk_s, l, caches, page_table)
        rows_s.append(rows)
    y_sample = xs[:nb][:, None, :]

    def stack(rows, i, shape):
        return jnp.stack([r[i] for r in rows], axis=0).reshape((depth,) + shape)

    lead_p, lead_s = (1, t), (nb, 1)
    return (y_prompt, y_sample,
            stack(rows_p, 0, lead_p + (DA_KV_HEADS, 2, DA_QK_DIM)), stack(rows_p, 1, lead_p + (DA_KV_HEADS, DA_V_DIM)),
            stack(rows_p, 2, lead_p + (DSA_KV_HEADS, DSA_HEAD_DIM)), stack(rows_p, 3, lead_p + (DSA_KV_HEADS, DSA_HEAD_DIM)),
            stack(rows_p, 4, lead_p + (IDX_DIM,)),
            stack(rows_s, 0, lead_s + (DA_KV_HEADS, 2, DA_QK_DIM)), stack(rows_s, 1, lead_s + (DA_KV_HEADS, DA_V_DIM)),
            stack(rows_s, 2, lead_s + (DSA_KV_HEADS, DSA_HEAD_DIM)), stack(rows_s, 3, lead_s + (DSA_KV_HEADS, DSA_HEAD_DIM)),
            stack(rows_s, 4, lead_s + (IDX_DIM,)))
```

```python
import functools
import math

import jax
import jax.numpy as jnp
import numpy as np
from jax import lax
from jax.experimental import pallas as pl
from jax.experimental.pallas import tpu as pltpu

F32 = jnp.float32
BF16 = jnp.bfloat16
I32 = jnp.int32

N_META = 16
ROPE_THETA = 500000.0
EPS = 1e-6
PAGE_SIZE = 128
DA_HEADS, DA_KV_HEADS, DA_QK_DIM, DA_V_DIM = 8, 4, 64, 128
DSA_HEADS, DSA_KV_HEADS, DSA_HEAD_DIM = 8, 4, 128
IDX_HEADS, IDX_DIM = 8, 64
TOPK_MAX = 256
SPLITS = (DA_HEADS * 2 * DA_QK_DIM, DA_KV_HEADS * 2 * DA_QK_DIM, DA_KV_HEADS * DA_V_DIM,
          DSA_HEADS * DSA_HEAD_DIM, DSA_KV_HEADS * DSA_HEAD_DIM, DSA_KV_HEADS * DSA_HEAD_DIM,
          IDX_HEADS * IDX_DIM, IDX_DIM, IDX_HEADS)

LANES = 128
MXU_COLS = 256
VMEM_LIMIT_BYTES = 56 * 1024 * 1024

PROJ_TN = 2 * MXU_COLS
_OFF = {}
_o = 0
for _name, _w in (("qa", 1024), ("ka", 512), ("va", 512), ("qd", 1024), ("kd", 512), ("vd", 512),
                  ("ga", 1024), ("gb", 1024), ("qi", 512), ("kw", PROJ_TN)):
    _OFF[_name] = (_o, _w)
    _o += _w
PROJ_N = _o

LOG2E = math.log2(math.e)
NEG = -1e30
INT_MIN = -2 ** 31

TQ = 512
TK = 512
TQ_SEL = 256
SHORTLIST = 12
T_ALIGN = 1536
TM_FFN = 768
TM_PROJ = 768
TM_MERGE = 512
FF_CHUNK = 256
PAGES_PER_STEP = 16
INDEX_PAGES_PER_STEP = 32


def _cparams(*sem):
    return pltpu.CompilerParams(dimension_semantics=sem, vmem_limit_bytes=VMEM_LIMIT_BYTES)


def _ffn_kernel(x_ref, g_ref, wg_ref, wu_ref, wd_ref, o_ref, h_scr):
    c = pl.program_id(1)

    @pl.when(c == 0)
    def _():
        x = x_ref[...]
        ms = jnp.mean(x * x, axis=-1, keepdims=True)
        h_scr[...] = ((x * lax.rsqrt(ms + EPS)) * g_ref[...]).astype(BF16)
        o_ref[...] = x

    h = h_scr[...]
    g = jnp.dot(h, wg_ref[...], preferred_element_type=F32)
    u = jnp.dot(h, wu_ref[...], preferred_element_type=F32)
    a = ((g * jax.nn.sigmoid(g)) * u).astype(BF16)
    o_ref[...] += 0.5 * jnp.dot(a, wd_ref[...], preferred_element_type=F32)


def _ffn(x, gain, w_gu, w_down, tm):
    m, d = x.shape
    d_ff = w_down.shape[0]
    n_c = d_ff // FF_CHUNK
    return pl.pallas_call(
        _ffn_kernel,
        out_shape=jax.ShapeDtypeStruct((m, d), F32),
        grid=(m // tm, n_c),
        in_specs=[
            pl.BlockSpec((tm, d), lambda i, c: (i, 0)),
            pl.BlockSpec((1, d), lambda i, c: (0, 0)),
            pl.BlockSpec((d, FF_CHUNK), lambda i, c: (0, c)),
            pl.BlockSpec((d, FF_CHUNK), lambda i, c: (0, c + n_c)),
            pl.BlockSpec((FF_CHUNK, d), lambda i, c: (c, 0)),
        ],
        out_specs=pl.BlockSpec((tm, d), lambda i, c: (i, 0)),
        scratch_shapes=[pltpu.VMEM((tm, d), BF16)],
        compiler_params=_cparams("parallel", "arbitrary"),
        name="ffn",
    )(x, gain.reshape(1, d), w_gu, w_gu, w_down)


def _seg_rms(x, gain, seg):
    cols = []
    for b in range(x.shape[1] // LANES):
        blk = x[:, b * LANES:(b + 1) * LANES]
        sq = blk * blk
        if seg == LANES:
            ms = jnp.sum(sq, axis=-1, keepdims=True) * (1.0 / seg)
        else:
            lane = lax.broadcasted_iota(I32, blk.shape, 1)
            lo = jnp.sum(jnp.where(lane < seg, sq, 0.0), axis=-1, keepdims=True)
            hi = jnp.sum(jnp.where(lane >= seg, sq, 0.0), axis=-1, keepdims=True)
            ms = jnp.where(lane < seg, lo, hi) * (1.0 / seg)
        cols.append((blk * lax.rsqrt(ms + EPS)) * gain)
    return jnp.concatenate(cols, axis=1)


def _rope(x, cos, sin, half):
    n = x.shape[1]
    reps = n // LANES
    c = jnp.concatenate([cos] * reps, axis=1)
    s = jnp.concatenate([sin] * reps, axis=1)
    up = pltpu.roll(x, n - half, 1)
    dn = pltpu.roll(x, half, 1)
    lane = lax.broadcasted_iota(I32, x.shape, 1)
    partner = jnp.where((lane & (2 * half - 1)) < half, up, dn)
    return x * c + partner * s


def _proj_kernel(qa_scale, qd_scale, x_ref, g_ref, w_ref, c64_ref, s64_ref, c128_ref, s128_ref,
                 qa_g, ka_g, qd_g, kd_g, o_ref, ob_ref, h_scr):
    j = pl.program_id(1)

    def emit(r, scale=None):
        o_ref[...] = r
        ob_ref[...] = (r if scale is None else r * scale).astype(BF16)

    @pl.when(j == 0)
    def _():
        x = x_ref[...]
        ms = jnp.mean(x * x, axis=-1, keepdims=True)
        h_scr[...] = ((x * lax.rsqrt(ms + EPS)) * g_ref[...]).astype(BF16)

    z = jnp.dot(h_scr[...], w_ref[...], preferred_element_type=F32)

    def tile_of(name):
        off, width = _OFF[name]
        return off // PROJ_TN, (off + width) // PROJ_TN

    def in_group(name):
        lo, hi = tile_of(name)
        return jnp.logical_and(j >= lo, j < hi)

    h64, h128 = DA_QK_DIM // 8, DSA_HEAD_DIM // 8

    @pl.when(in_group("qa"))
    def _():
        emit(_rope(_seg_rms(z, qa_g[...], DA_QK_DIM), c64_ref[...], s64_ref[...], h64), qa_scale)

    @pl.when(in_group("ka"))
    def _():
        emit(_rope(_seg_rms(z, ka_g[...], DA_QK_DIM), c64_ref[...], s64_ref[...], h64))

    @pl.when(in_group("qd"))
    def _():
        emit(_rope(_seg_rms(z, qd_g[...], DSA_HEAD_DIM), c128_ref[...], s128_ref[...], h128), qd_scale)

    @pl.when(in_group("kd"))
    def _():
        emit(_rope(_seg_rms(z, kd_g[...], DSA_HEAD_DIM), c128_ref[...], s128_ref[...], h128))

    @pl.when(jnp.logical_or(in_group("va"), in_group("vd")))
    def _():
        emit(z)

    @pl.when(in_group("qi"))
    def _():
        emit(_rope(z, c64_ref[...], s64_ref[...], h64))

    @pl.when(jnp.logical_or(in_group("ga"), in_group("gb")))
    def _():
        emit(jax.nn.sigmoid(z))

    @pl.when(in_group("kw"))
    def _():
        r = _rope(z, c64_ref[...], s64_ref[...], h64)
        lane = lax.broadcasted_iota(I32, z.shape, 1)
        wi_scale = IDX_HEADS ** -0.5 * IDX_DIM ** -0.5
        emit(jnp.where(lane < IDX_DIM, r, z * wi_scale))


def _proj(x, gain, w_packed, tabs, gains, tm, qa_scale, qd_scale):
    m, d = x.shape
    n = w_packed.shape[1]
    c64, s64, c128, s128 = tabs
    row = lambda i, j: (i, 0)
    const = lambda i, j: (0, 0)
    return pl.pallas_call(
        functools.partial(_proj_kernel, qa_scale, qd_scale),
        out_shape=(jax.ShapeDtypeStruct((m, n), F32), jax.ShapeDtypeStruct((m, n), BF16)),
        grid=(m // tm, n // PROJ_TN),
        in_specs=[
            pl.BlockSpec((tm, d), row),
            pl.BlockSpec((1, d), const),
            pl.BlockSpec((d, PROJ_TN), lambda i, j: (0, j)),
            pl.BlockSpec((tm, LANES), row), pl.BlockSpec((tm, LANES), row),
            pl.BlockSpec((tm, LANES), row), pl.BlockSpec((tm, LANES), row),
            pl.BlockSpec((1, LANES), const), pl.BlockSpec((1, LANES), const),
            pl.BlockSpec((1, LANES), const), pl.BlockSpec((1, LANES), const),
        ],
        out_specs=(pl.BlockSpec((tm, PROJ_TN), lambda i, j: (i, j)),
                   pl.BlockSpec((tm, PROJ_TN), lambda i, j: (i, j))),
        scratch_shapes=[pltpu.VMEM((tm, d), BF16)],
        compiler_params=_cparams("parallel", "arbitrary"),
        name="proj",
    )(x, gain.reshape(1, d), w_packed, c64, s64, c128, s128, *gains)


def _merge_kernel(lam_init, x_ref, oa_ref, od_ref, ga_ref, gb_ref, g_ref, wa_ref, wb_ref, wo_ref,
                  o_ref, a_scr):
    for h in range(DA_HEADS):
        blk = oa_ref[:, h * DA_V_DIM:(h + 1) * DA_V_DIM]
        ms = jnp.mean(blk * blk, axis=-1, keepdims=True)
        y = ((blk * lax.rsqrt(ms + EPS)) * g_ref[...]) * (1.0 - lam_init)
        a_scr[:, h * DA_V_DIM:(h + 1) * DA_V_DIM] = y.astype(BF16)
    a = jnp.dot(a_scr[...], wa_ref[...], preferred_element_type=F32)
    b = jnp.dot(od_ref[...].astype(BF16), wb_ref[...], preferred_element_type=F32)
    merged = ga_ref[...] * a + gb_ref[...] * b
    o_ref[...] = x_ref[...] + jnp.dot(merged.astype(BF16), wo_ref[...], preferred_element_type=F32)


def _merge(x, oa, od, z, gain, wa, wb, wo, lam_init, tm):
    m, d = x.shape
    row = lambda i: (i, 0)
    const = lambda i: (0, 0)
    ga_blk = _OFF["ga"][0] // d
    gb_blk = _OFF["gb"][0] // d
    return pl.pallas_call(
        functools.partial(_merge_kernel, lam_init),
        out_shape=jax.ShapeDtypeStruct((m, d), F32),
        grid=(m // tm,),
        in_specs=[
            pl.BlockSpec((tm, d), row), pl.BlockSpec((tm, d), row), pl.BlockSpec((tm, d), row),
            pl.BlockSpec((tm, d), lambda i: (i, ga_blk)), pl.BlockSpec((tm, d), lambda i: (i, gb_blk)),
            pl.BlockSpec((1, DA_V_DIM), const),
            pl.BlockSpec((d, d), const), pl.BlockSpec((d, d), const), pl.BlockSpec((d, d), const),
        ],
        out_specs=pl.BlockSpec((tm, d), row),
        scratch_shapes=[pltpu.VMEM((tm, d), BF16)],
        compiler_params=_cparams("parallel"),
        name="merge",
    )(x, oa, od, z, z, gain.reshape(1, DA_V_DIM), wa, wb, wo)


def _lambda_of(lp, lam_init):
    a = jnp.sum(lp[0:1, :] * lp[1:2, :], axis=-1, keepdims=True)
    b = jnp.sum(lp[2:3, :] * lp[3:4, :], axis=-1, keepdims=True)
    return jnp.exp(a) - jnp.exp(b) + lam_init


def _softmax_step(s, v, m_scr, l_scr, acc_scr):
    m_prev = m_scr[...]
    m_new = jnp.maximum(m_prev, jnp.max(s, axis=-1, keepdims=True))
    alpha = jnp.exp(m_prev - m_new)
    p = jnp.exp(s - m_new)
    l_scr[...] = alpha * l_scr[...] + jnp.sum(p, axis=-1, keepdims=True)
    acc_scr[...] = alpha * acc_scr[...] + jnp.dot(p.astype(BF16), v, preferred_element_type=F32)
    m_scr[...] = m_new


def _flash_step(s, v_ext, m_ref, acc_ref):
    m_prev = m_ref[...]
    m_new = jnp.maximum(m_prev, jnp.max(s, axis=-1, keepdims=True))
    alpha = jnp.exp2(m_prev - m_new)
    p = jnp.exp2(s - jnp.concatenate([m_new] * (s.shape[1] // LANES), axis=1))
    acc_ref[...] = (jnp.concatenate([alpha] * (acc_ref.shape[1] // LANES), axis=1) * acc_ref[...]
                    + jnp.dot(p.astype(BF16), v_ext, preferred_element_type=F32))
    m_ref[...] = m_new


def _diff_attn_kernel(lam_init, tail, q_ref, kt_ref, v_ref, lp_ref, o_ref, lhs_scr, m_scr, acc_scr):
    qt = pl.program_id(1)

    def run(tq):
        rows = 4 * tq
        q = q_ref[0:tq, :]
        lane = lax.broadcasted_iota(I32, (tq, LANES), 1)
        for g in range(2):
            qg = q[:, g * LANES:(g + 1) * LANES]
            for mp in range(2):
                keep = (lane < DA_QK_DIM) if mp == 0 else (lane >= DA_QK_DIM)
                r = (g * 2 + mp) * tq
                lhs_scr[r:r + tq, :] = jnp.where(keep, qg, jnp.zeros_like(qg))
        m_ref, acc_ref = m_scr.at[0:rows], acc_scr.at[0:rows]
        m_ref[...] = jnp.full((rows, LANES), NEG, F32)
        acc_ref[...] = jnp.zeros((rows, 2 * DA_V_DIM), F32)

        def step(kt, masked):
            s = jnp.dot(lhs_scr[0:rows, :], kt_ref[0, kt], preferred_element_type=F32)
            if masked:
                qpos = qt * TQ + (lax.broadcasted_iota(I32, (rows, TK), 0) & (tq - 1))
                col = lax.broadcasted_iota(I32, (rows, TK), 1) + kt * TK
                s = jnp.where(col <= qpos, s, NEG)
            v = v_ref[pl.ds(pl.multiple_of(kt * TK, TK), TK), :]
            _flash_step(s, v, m_ref, acc_ref)

        def body(kt, carry):
            step(kt, False)
            return carry

        lax.fori_loop(0, qt, body, 0)
        step(qt, True)
        acc = acc_ref[...]
        o = acc[:, :DA_V_DIM] / acc[:, DA_V_DIM:]
        lam = _lambda_of(lp_ref[...], lam_init)
        for g in range(2):
            r = g * 2 * tq
            o_ref[0:tq, g * DA_V_DIM:(g + 1) * DA_V_DIM] = o[r:r + tq, :] - lam * o[r + tq:r + 2 * tq, :]
        if tq < TQ:
            o_ref[tq:TQ, :] = jnp.zeros((TQ - tq, o_ref.shape[1]), F32)

    if tail is None:
        run(TQ)
    else:
        pl.when(qt != tail[0])(lambda: run(TQ))
        pl.when(qt == tail[0])(lambda: run(tail[1]))


def _tail_tile(t_real, t_pad):
    rem = t_real % TQ
    if t_pad - t_real >= TQ or rem == 0 or rem > TQ // 4:
        return None
    return t_real // TQ, max(16, pl.next_power_of_2(rem))


def _diff_attn(q, kt, v_ext, lp, lam_init, t_real):
    t = q.shape[0]
    n_kt = t // TK
    assert TQ == TK
    return pl.pallas_call(
        functools.partial(_diff_attn_kernel, lam_init, _tail_tile(t_real, t)),
        out_shape=jax.ShapeDtypeStruct((t, DA_HEADS * DA_V_DIM), F32),
        grid=(DA_KV_HEADS, t // TQ),
        in_specs=[
            pl.BlockSpec((TQ, 2 * LANES), lambda kv, i: (i, kv)),
            pl.BlockSpec((1, n_kt, LANES, TK), lambda kv, i: (kv, 0, 0, 0)),
            pl.BlockSpec((t, 2 * DA_V_DIM), lambda kv, i: (0, kv)),
            pl.BlockSpec((4, DA_QK_DIM), lambda kv, i: (0, 0)),
        ],
        out_specs=pl.BlockSpec((TQ, 2 * DA_V_DIM), lambda kv, i: (i, kv)),
        scratch_shapes=[pltpu.VMEM((4 * TQ, LANES), BF16), pltpu.VMEM((4 * TQ, LANES), F32),
                        pltpu.VMEM((4 * TQ, 2 * DA_V_DIM), F32)],
        compiler_params=_cparams("parallel", "arbitrary"),
        name="diff_attn",
    )(q, kt, v_ext, lp)


def _float_key(s):
    s = jnp.where(s == 0.0, 0.0, s)
    bits = pltpu.bitcast(s, I32)
    return jnp.where(bits < 0, bits ^ jnp.int32(0x7FFFFFFF), bits)


def _key_float(key):
    bits = jnp.where(key < 0, key ^ jnp.int32(0x7FFFFFFF), key)
    return pltpu.bitcast(bits, F32)


KEY_LOWEST_FINITE = INT_MIN + 0x00800000


def _kth_largest_key(count_ge, rows, k):
    def bit_body(i, t):
        cand = t + lax.shift_left(jnp.int32(1), 31 - i)
        return jnp.where(count_ge(cand) >= k, cand, t)

    return lax.fori_loop(0, 32, bit_body, jnp.full((rows, 1), INT_MIN, I32))


def _select_kernel(topk, t_real, qi_ref, wi_ref, ki_ref, o_ref, lhs_scr, sc_scr, short_scr, t_scr, n_scr):
    qt = pl.program_id(0)
    n_kt = sc_scr.shape[0]
    n_rows = jnp.where(qt * TQ_SEL < t_real, jnp.minimum((qt + 1) * TQ_SEL, t_real), 1)
    n_k = (n_rows + TK - 1) // TK
    lane = lax.broadcasted_iota(I32, (TQ_SEL, LANES), 1)
    for h in range(IDX_HEADS):
        pair = qi_ref[:, (h // 2) * LANES:(h // 2 + 1) * LANES]
        keep = (lane < IDX_DIM) if h % 2 == 0 else (lane >= IDX_DIM)
        lhs_scr[h * TQ_SEL:(h + 1) * TQ_SEL, :] = jnp.where(keep, pair, jnp.zeros_like(pair))
    wi = wi_ref[...]
    w_cols = [wi[:, h:h + 1] for h in range(IDX_HEADS)]
    qpos = qt * TQ_SEL + lax.broadcasted_iota(I32, (TQ_SEL, TK), 0)
    qpos = jnp.where(qpos < t_real, qpos, 0)
    col = lax.broadcasted_iota(I32, (TQ_SEL, TK), 1)

    def score_body(kt, carry):
        d = jnp.dot(lhs_scr[...], ki_ref[kt], preferred_element_type=F32)
        sc = jnp.zeros((TQ_SEL, TK), F32)
        for h in range(IDX_HEADS):
            sc = sc + jnp.maximum(d[h * TQ_SEL:(h + 1) * TQ_SEL, :], 0.0) * w_cols[h]
        sc = jnp.where(sc == 0.0, 0.0, sc)
        sc_scr[kt] = jnp.where(col + kt * TK <= qpos, sc, -jnp.inf)
        return carry

    lax.fori_loop(0, n_k, score_body, 0)

    def count(pred):
        def body(kt, acc):
            hit = jnp.where(pred(sc_scr[kt], col + kt * TK), 1.0, 0.0)
            for c in range(TK // LANES):
                acc = acc + hit[:, c * LANES:(c + 1) * LANES]
            return acc

        acc = lax.fori_loop(0, n_k, body, jnp.zeros((TQ_SEL, LANES), F32))
        return jnp.sum(acc, axis=-1, keepdims=True)

    def shortlist_body(rg, carry):
        rows8 = pl.ds(pl.multiple_of(rg * 8, 8), 8)

        def insert_body(kt, tops):
            for c in range(TK // LANES):
                x = sc_scr[kt, rows8, c * LANES:(c + 1) * LANES]
                new = []
                for top in tops:
                    new.append(jnp.maximum(top, x))
                    x = jnp.minimum(top, x)
                tops = tuple(new)
            return tops

        tops = lax.fori_loop(0, n_k, insert_body, (jnp.full((8, LANES), -jnp.inf, F32),) * SHORTLIST)
        for j in range(SHORTLIST):
            short_scr[j, rows8, :] = _float_key(tops[j])
        return carry

    lax.fori_loop(0, TQ_SEL // 8, shortlist_body, 0)

    def count_short(cand):
        acc = jnp.zeros((TQ_SEL, LANES), F32)
        for j in range(SHORTLIST):
            acc = acc + jnp.where(short_scr[j] >= cand, 1.0, 0.0)
        return jnp.sum(acc, axis=-1, keepdims=True)

    kf = float(topk)
    t_key = jnp.maximum(_kth_largest_key(count_short, TQ_SEL, kf), KEY_LOWEST_FINITE)
    n_ge = count(lambda s, idx: s >= _key_float(t_key))
    t_scr[...] = t_key
    n_scr[...] = n_ge

    @pl.when(jnp.max(jnp.abs(n_ge - count_short(t_key))) > 0.0)
    def _():
        t_full = _kth_largest_key(lambda cand: count(lambda s, idx: s >= _key_float(cand)), TQ_SEL, kf)
        t_full = jnp.maximum(t_full, KEY_LOWEST_FINITE)
        t_scr[...] = t_full
        n_scr[...] = count(lambda s, idx: s >= _key_float(t_full))

    t = _key_float(t_scr[...])
    n_ge = n_scr[...]

    @pl.when(jnp.max(n_ge) > kf)
    def _():
        need = kf - count(lambda s, idx: s > t)

        def bit_body(i, x):
            cand = x + lax.shift_left(jnp.int32(1), 14 - i)
            below = count(lambda s, idx: jnp.logical_and(s == t, idx < cand))
            return jnp.where(below < need, cand, x)

        last = lax.fori_loop(0, 15, bit_body, jnp.zeros((TQ_SEL, 1), I32))

        def drop_body(kt, carry):
            s = sc_scr[kt]
            drop = jnp.logical_and(s == t, col + kt * TK > last)
            sc_scr[kt] = jnp.where(drop, -jnp.inf, s)
            return carry

        lax.fori_loop(0, n_k, drop_body, 0)

    def write_body(kt, carry):
        o_ref[kt] = jnp.where(sc_scr[kt] >= t, 0.0, NEG).astype(BF16)
        return carry

    lax.fori_loop(0, n_k, write_body, 0)

    def fill_body(kt, carry):
        o_ref[kt] = jnp.full((TQ_SEL, TK), NEG, BF16)
        return carry

    lax.fori_loop(n_k, n_kt, fill_body, 0)


def _select(qi, wi, ki2, topk, t_real):
    t = qi.shape[0]
    n_kt = t // TK
    assert t < 2 ** 15
    return pl.pallas_call(
        functools.partial(_select_kernel, topk, t_real),
        out_shape=jax.ShapeDtypeStruct((n_kt, t, TK), BF16),
        grid=(t // TQ_SEL,),
        in_specs=[
            pl.BlockSpec((TQ_SEL, IDX_HEADS * IDX_DIM), lambda i: (i, 0)),
            pl.BlockSpec((TQ_SEL, LANES), lambda i: (i, 0)),
            pl.BlockSpec((n_kt, LANES, TK), lambda i: (0, 0, 0)),
        ],
        out_specs=pl.BlockSpec((n_kt, TQ_SEL, TK), lambda i: (0, i, 0)),
        scratch_shapes=[pltpu.VMEM((IDX_HEADS * TQ_SEL, LANES), BF16),
                        pltpu.VMEM((n_kt, TQ_SEL, TK), F32),
                        pltpu.VMEM((SHORTLIST, TQ_SEL, LANES), I32),
                        pltpu.VMEM((TQ_SEL, 1), I32), pltpu.VMEM((TQ_SEL, 1), F32)],
        compiler_params=_cparams("parallel"),
        name="dsa_select",
    )(qi, wi, ki2)


def _dsa_attn_kernel(tail, qt_ref, kt_ref_s, q_ref, kt_ref, v_ref, b_ref, o_ref, lhs_scr, m_scr, acc_scr):
    step = pl.program_id(0)
    qt, kt = qt_ref[step], kt_ref_s[step]

    def run(tq):
        rows = 2 * tq

        @pl.when(kt == 0)
        def _():
            for kv in range(DSA_KV_HEADS):
                for g in range(2):
                    h = kv * 2 + g
                    lhs_scr[kv, g * tq:(g + 1) * tq, :] = q_ref[0:tq, h * LANES:(h + 1) * LANES]
            m_scr[:, 0:rows, :] = jnp.full((DSA_KV_HEADS, rows, LANES), NEG, F32)
            acc_scr[:, 0:rows, :] = jnp.zeros((DSA_KV_HEADS, rows, 2 * DSA_HEAD_DIM), F32)

        bias = b_ref[0, 0:tq, :].astype(F32)
        bias2 = jnp.concatenate([bias, bias], axis=0)
        for kv in range(DSA_KV_HEADS):
            s = jnp.dot(lhs_scr[kv, 0:rows, :], kt_ref[kv, 0], preferred_element_type=F32) + bias2
            v = v_ref[:, kv * 2 * DSA_HEAD_DIM:(kv + 1) * 2 * DSA_HEAD_DIM]
            _flash_step(s, v, m_scr.at[kv, 0:rows], acc_scr.at[kv, 0:rows])

        @pl.when(kt == qt)
        def _():
            for kv in range(DSA_KV_HEADS):
                acc = acc_scr[kv, 0:rows, :]
                o = acc[:, :DSA_HEAD_DIM] / acc[:, DSA_HEAD_DIM:]
                for g in range(2):
                    h = kv * 2 + g
                    o_ref[0:tq, h * DSA_HEAD_DIM:(h + 1) * DSA_HEAD_DIM] = o[g * tq:(g + 1) * tq, :]
            if tq < TQ:
                o_ref[tq:TQ, :] = jnp.zeros((TQ - tq, o_ref.shape[1]), F32)

    if tail is None:
        run(TQ)
    else:
        pl.when(qt != tail[0])(lambda: run(TQ))
        pl.when(qt == tail[0])(lambda: run(tail[1]))


def _dsa_attn(q, kt, v_ext, bias, t_real):
    t = q.shape[0]
    assert TQ == TK
    kv, d = DSA_KV_HEADS, DSA_HEAD_DIM
    n_t = t // TQ
    qt_of_step = np.concatenate([np.full(i + 1, i, np.int32) for i in range(n_t)])
    kt_of_step = np.concatenate([np.arange(i + 1, dtype=np.int32) for i in range(n_t)])
    grid_spec = pltpu.PrefetchScalarGridSpec(
        num_scalar_prefetch=2,
        grid=(qt_of_step.shape[0],),
        in_specs=[
            pl.BlockSpec((TQ, DSA_HEADS * d), lambda s, qt, kt: (qt[s], 0)),
            pl.BlockSpec((kv, 1, LANES, TK), lambda s, qt, kt: (0, kt[s], 0, 0)),
            pl.BlockSpec((TK, kv * 2 * d), lambda s, qt, kt: (kt[s], 0)),
            pl.BlockSpec((1, TQ, TK), lambda s, qt, kt: (kt[s], qt[s], 0)),
        ],
        out_specs=pl.BlockSpec((TQ, DSA_HEADS * d), lambda s, qt, kt: (qt[s], 0)),
        scratch_shapes=[pltpu.VMEM((kv, 2 * TQ, LANES), BF16), pltpu.VMEM((kv, 2 * TQ, LANES), F32),
                        pltpu.VMEM((kv, 2 * TQ, 2 * d), F32)],
    )
    return pl.pallas_call(
        functools.partial(_dsa_attn_kernel, _tail_tile(t_real, t)),
        out_shape=jax.ShapeDtypeStruct((t, DSA_HEADS * d), F32),
        grid_spec=grid_spec,
        compiler_params=_cparams("arbitrary"),
        name="dsa_attn",
    )(jnp.asarray(qt_of_step), jnp.asarray(kt_of_step), q, kt, v_ext, bias)


STREAM_ROWS = 16


def _paged_attn_kernel(mode, lam_init, n_pages, npg, *refs):
    pt_ref, wq_ref = refs[0], refs[1]
    k_refs = refs[2:2 + npg]
    v_refs = refs[2 + npg:2 + 2 * npg]
    knew_ref, vnew_ref = refs[2 + 2 * npg], refs[3 + 2 * npg]
    pos = 4 + 2 * npg
    if mode == "da":
        lp_ref = refs[pos]
        pos += 1
    else:
        bias_ref, bnew_ref = refs[pos], refs[pos + 1]
        pos += 2
    o_ref, m_scr, l_scr, acc_scr = refs[pos:pos + 4]
    j = pl.program_id(1)
    wq = wq_ref[0]

    @pl.when(j == 0)
    def _():
        s_new = jnp.sum(wq.astype(F32) * knew_ref[0], axis=-1, keepdims=True)
        if mode == "dsa":
            s_new = s_new + bnew_ref[0][:, 0:1].astype(F32)
        m_scr[...] = s_new
        l_scr[...] = jnp.ones(l_scr.shape, F32)
        acc_scr[...] = jnp.broadcast_to(vnew_ref[0], acc_scr.shape)

    def token_rows(ref):
        n_kv = ref.shape[2] // PAGE_SIZE
        return jnp.concatenate([ref[0, 0, pl.ds(kv, PAGE_SIZE, stride=n_kv), :] for kv in range(n_kv)], axis=1)

    v = jnp.concatenate([token_rows(r) for r in v_refs], axis=0).astype(BF16)
    if mode == "da":
        kt = jnp.concatenate([r[0, 0].reshape(-1, PAGE_SIZE) for r in k_refs], axis=1).astype(BF16)
        s = jnp.dot(wq, kt, preferred_element_type=F32)
    else:
        k = jnp.concatenate([token_rows(r) for r in k_refs], axis=0).astype(BF16)
        s = lax.dot_general(wq, k, (((1,), (1,)), ((), ())), preferred_element_type=F32)
        s = s + bias_ref[0].astype(F32)
    _softmax_step(s, v, m_scr, l_scr, acc_scr)

    @pl.when(j == n_pages // npg - 1)
    def _():
        o = acc_scr[...] / l_scr[...]
        if mode == "da":
            lam = _lambda_of(lp_ref[...], lam_init)
            for kv in range(DA_KV_HEADS):
                for g in range(2):
                    r = kv * 4 + g * 2
                    blk = slice(kv * DA_V_DIM, (kv + 1) * DA_V_DIM)
                    h = kv * 2 + g
                    o_ref[0, :, h * DA_V_DIM:(h + 1) * DA_V_DIM] = o[r:r + 1, blk] - lam * o[r + 1:r + 2, blk]
        else:
            for kv in range(DSA_KV_HEADS):
                for g in range(2):
                    r = kv * 2 + g
                    o_ref[0, :, r * DSA_HEAD_DIM:(r + 1) * DSA_HEAD_DIM] = (
                        o[r:r + 1, kv * DSA_HEAD_DIM:(kv + 1) * DSA_HEAD_DIM])


def _paged_attn(mode, layer, page_table, wq, k_cache, v_cache, k_new, v_new, extra, lam_init):
    nb, n_pages = page_table.shape
    npg = math.gcd(PAGES_PER_STEP, n_pages)
    width = k_new.shape[2]

    def page_spec(cache, i):
        tail = cache.shape[2:]
        zeros = (0,) * len(tail)
        return pl.BlockSpec((1, 1) + tail, lambda b, j, pt: (layer, pt[b * n_pages + j * npg + i]) + zeros)

    per_seq = lambda b, j, pt: (b, 0, 0)
    in_specs = [pl.BlockSpec((1, STREAM_ROWS, width), per_seq)]
    in_specs += [page_spec(k_cache, i) for i in range(npg)] + [page_spec(v_cache, i) for i in range(npg)]
    in_specs += [pl.BlockSpec((1, 1, width), per_seq), pl.BlockSpec((1, 1, width), per_seq)]
    args = [wq] + [k_cache] * npg + [v_cache] * npg + [k_new, v_new]
    if mode == "da":
        in_specs.append(pl.BlockSpec((4, DA_QK_DIM), lambda b, j, pt: (0, 0)))
        args.append(extra)
    else:
        in_specs.append(pl.BlockSpec((1, 1, npg * PAGE_SIZE), lambda b, j, pt: (b, 0, j)))
        in_specs.append(pl.BlockSpec((1, 1, PAGE_SIZE), lambda b, j, pt: (b, 0, n_pages)))
        args += [extra, extra]
    grid_spec = pltpu.PrefetchScalarGridSpec(
        num_scalar_prefetch=1,
        grid=(nb, n_pages // npg),
        in_specs=in_specs,
        out_specs=pl.BlockSpec((1, 1, 8 * LANES), per_seq),
        scratch_shapes=[pltpu.VMEM((STREAM_ROWS, 1), F32), pltpu.VMEM((STREAM_ROWS, 1), F32),
                        pltpu.VMEM((STREAM_ROWS, width), F32)],
    )
    return pl.pallas_call(
        functools.partial(_paged_attn_kernel, mode, lam_init, n_pages, npg),
        out_shape=jax.ShapeDtypeStruct((nb, 1, 8 * LANES), F32),
        grid_spec=grid_spec,
        compiler_params=_cparams("parallel", "arbitrary"),
        name="paged_" + mode,
    )(page_table.reshape(-1), *args)


def _paged_index_kernel(*refs):
    npg = len(refs) - 4
    q_ref, w_ref = refs[1], refs[2]
    k_refs = refs[3:3 + npg]
    o_ref = refs[3 + npg]
    kt = jnp.concatenate([r[0, 0] for r in k_refs], axis=1).astype(BF16)
    d = jnp.dot(q_ref[0], kt, preferred_element_type=F32)
    o_ref[0] = jnp.sum(jnp.maximum(d, 0.0) * w_ref[0], axis=0, keepdims=True)


def _paged_index(layer, page_table, qi, wi, k_cache):
    nb, n_pages = page_table.shape
    npg = math.gcd(INDEX_PAGES_PER_STEP, n_pages)

    def page_spec(i):
        return pl.BlockSpec((1, 1, IDX_DIM, PAGE_SIZE),
                            lambda b, j, pt: (layer, pt[b * n_pages + j * npg + i], 0, 0))

    per_seq = lambda b, j, pt: (b, 0, 0)
    grid_spec = pltpu.PrefetchScalarGridSpec(
        num_scalar_prefetch=1,
        grid=(nb, n_pages // npg),
        in_specs=[pl.BlockSpec((1, IDX_HEADS, IDX_DIM), per_seq), pl.BlockSpec((1, IDX_HEADS, 1), per_seq)]
        + [page_spec(i) for i in range(npg)],
        out_specs=pl.BlockSpec((1, 1, npg * PAGE_SIZE), lambda b, j, pt: (b, 0, j)),
    )
    return pl.pallas_call(
        _paged_index_kernel,
        out_shape=jax.ShapeDtypeStruct((nb, 1, n_pages * PAGE_SIZE), F32),
        grid_spec=grid_spec,
        compiler_params=_cparams("parallel", "arbitrary"),
        name="paged_index",
    )(page_table.reshape(-1), qi, wi, *([k_cache] * npg))


def _sample_select_kernel(topk, sc_ref, qi_ref, kin_ref, wi_ref, o_ref, key_scr):
    nb, past = sc_ref.shape
    width = past + LANES
    prod = qi_ref[...] * kin_ref[...]
    lane = lax.broadcasted_iota(I32, (nb, LANES), 1)
    wi = wi_ref[...]
    s_new = jnp.zeros((nb, 1), F32)
    for h in range(IDX_HEADS):
        blk = prod[:, (h // 2) * LANES:(h // 2 + 1) * LANES]
        keep = (lane < IDX_DIM) if h % 2 == 0 else (lane >= IDX_DIM)
        d = jnp.sum(jnp.where(keep, blk, 0.0), axis=-1, keepdims=True)
        s_new = s_new + jnp.maximum(d, 0.0) * wi[:, h:h + 1]
    key_scr[:, 0:past] = _float_key(sc_ref[...])
    key_scr[:, past:width] = jnp.where(lane == 0, _float_key(s_new), INT_MIN)
    col = lax.broadcasted_iota(I32, (nb, width), 1)

    def count(pred):
        return jnp.sum(jnp.where(pred(key_scr[...], col), 1.0, 0.0), axis=-1, keepdims=True)

    kf = float(topk)
    t = _kth_largest_key(lambda cand: count(lambda keys, idx: keys >= cand), nb, kf)
    t = jnp.maximum(t, INT_MIN + 1)
    n_ge = count(lambda keys, idx: keys >= t)

    @pl.when(jnp.max(n_ge) > kf)
    def _():
        need = kf - count(lambda keys, idx: keys > t)

        def bit_body(i, x):
            cand = x + lax.shift_left(jnp.int32(1), 14 - i)
            below = count(lambda keys, idx: jnp.logical_and(keys == t, idx < cand))
            return jnp.where(below < need, cand, x)

        last = lax.fori_loop(0, 15, bit_body, jnp.zeros((nb, 1), I32))
        keys = key_scr[...]
        key_scr[...] = jnp.where(jnp.logical_and(keys == t, col > last), INT_MIN, keys)

    o_ref[...] = jnp.where(key_scr[...] >= t, 0.0, NEG).astype(BF16)


def _sample_select(scores, qi, ki_new8, wi, topk):
    nb, past = scores.shape
    assert past + LANES < 2 ** 15
    return pl.pallas_call(
        functools.partial(_sample_select_kernel, topk),
        out_shape=jax.ShapeDtypeStruct((nb, past + LANES), BF16),
        scratch_shapes=[pltpu.VMEM((nb, past + LANES), I32)],
        compiler_params=pltpu.CompilerParams(vmem_limit_bytes=VMEM_LIMIT_BYTES),
        name="sample_select",
    )(scores, qi, ki_new8, wi)


def _rope_tables(pos, d):
    r = d // 4
    half = r // 2
    inv = 1.0 / (ROPE_THETA ** (jnp.arange(half, dtype=F32) * (2.0 / r)))
    ang = pos.astype(F32)[:, None] * inv[None, :]
    cos, sin = jnp.cos(ang), jnp.sin(ang)
    n = pos.shape[0]
    c = jnp.concatenate([cos, cos, jnp.ones((n, d - r), F32)], axis=1)
    s = jnp.concatenate([-sin, sin, jnp.zeros((n, d - r), F32)], axis=1)
    reps = LANES // d
    return jnp.tile(c, (1, reps)), jnp.tile(s, (1, reps))


def _pack_w_in(w):
    parts = jnp.split(w, np.cumsum(SPLITS).tolist(), axis=1)
    qa, ka, va, qd, kd, vd, qi, ki, wi, gates = parts
    ga, gb = jnp.split(gates, 2, axis=1)
    pad = jnp.zeros((w.shape[0], PROJ_TN - IDX_DIM - IDX_HEADS), w.dtype)
    return jnp.concatenate([qa, ka, va, qd, kd, vd, ga, gb, qi, ki, wi, pad], axis=1).astype(BF16)


def _cols(z, name):
    off, width = _OFF[name]
    return z[:, off:off + width]


def _key_tiles(k, n_heads):
    t = k.shape[0]
    kt = k.reshape(t // TK, TK, n_heads, LANES)
    return jnp.transpose(kt, (2, 0, 3, 1))


def _with_ones(v, n_heads):
    t = v.shape[0]
    v3 = v.reshape(t, n_heads, LANES)
    return jnp.concatenate([v3, jnp.ones_like(v3)], axis=2).reshape(t, n_heads * 2 * LANES)


def _layer_weights(l, p):
    return dict(
        norm_ffn1=p["norm_ffn1"][l], w_gu1=p["ffn1_w_gu"][l].astype(BF16), w_d1=p["ffn1_w_down"][l].astype(BF16),
        norm_mix=p["norm_mix"][l], w_in=_pack_w_in(p["w_in"][l]),
        gains=tuple(jnp.tile(p[n][l], LANES // p[n].shape[1]).reshape(1, LANES)
                    for n in ("da_q_norm", "da_k_norm", "dsa_q_norm", "dsa_k_norm")),
        lp=p["da_lambda"][l], out_norm=p["da_out_norm"][l],
        wa=p["w_branch_a"][l].astype(BF16), wb=p["w_branch_b"][l].astype(BF16), wo=p["w_out"][l].astype(BF16),
        norm_ffn2=p["norm_ffn2"][l], w_gu2=p["ffn2_w_gu"][l].astype(BF16), w_d2=p["ffn2_w_down"][l].astype(BF16),
    )


def _prompt_layer(x, w, tabs, lam_init, topk, t_real):
    x = _ffn(x, w["norm_ffn1"], w["w_gu1"], w["w_d1"], TM_FFN)
    z, zb = _proj(x, w["norm_mix"], w["w_in"], tabs, w["gains"], TM_PROJ,
                  DA_QK_DIM ** -0.5 * LOG2E, DSA_HEAD_DIM ** -0.5 * LOG2E)
    ka, va, kd, vd = _cols(z, "ka"), _cols(z, "va"), _cols(z, "kd"), _cols(z, "vd")
    kw = _cols(z, "kw")
    ki = kw[:, :IDX_DIM]
    oa = _diff_attn(_cols(zb, "qa"), _key_tiles(_cols(zb, "ka"), DA_KV_HEADS),
                    _with_ones(_cols(zb, "va"), DA_KV_HEADS), w["lp"], lam_init, t_real)
    kib = _cols(zb, "kw")[:, :IDX_DIM]
    ki2 = _key_tiles(jnp.concatenate([kib, kib], axis=1), 1)[0]
    wi = kw[:, IDX_DIM:IDX_DIM + LANES]
    bias = _select(_cols(zb, "qi"), wi, ki2, topk, t_real)
    od = _dsa_attn(_cols(zb, "qd"), _key_tiles(_cols(zb, "kd"), DSA_KV_HEADS),
                   _with_ones(_cols(zb, "vd"), DSA_KV_HEADS), bias, t_real)
    x = _merge(x, oa, od, z, w["out_norm"], w["wa"], w["wb"], w["wo"], lam_init, TM_MERGE)
    x = _ffn(x, w["norm_ffn2"], w["w_gu2"], w["w_d2"], TM_FFN)
    return x, (ka, va, kd, vd, ki)


def _expand_queries(q, n_kv, per_kv_rows, width, scale):
    nb = q.shape[0]
    eye = jnp.eye(n_kv, dtype=q.dtype)
    out = jnp.einsum("bkrw,kc->bkrcw", q * scale, eye).reshape(nb, n_kv * per_kv_rows, n_kv * width)
    pad = STREAM_ROWS - n_kv * per_kv_rows
    return jnp.pad(out, ((0, 0), (0, pad), (0, 0))).astype(BF16)


def _cache_views(cache_da_k, cache_da_v, cache_dsa_k, cache_dsa_v, cache_idx_k):
    depth, n_pool = cache_da_k.shape[:2]
    rows = lambda c: c.reshape(depth, n_pool, -1, c.shape[-1])
    return (jnp.transpose(cache_da_k, (0, 1, 3, 4, 5, 2)), rows(cache_da_v), rows(cache_dsa_k),
            rows(cache_dsa_v), jnp.transpose(cache_idx_k, (0, 1, 3, 2)))


def _sample_layer(x, w, tabs, lam_init, topk, layer, caches, page_table):
    da_kt, da_v, dsa_k, dsa_v, idx_kt = caches
    nb = page_table.shape[0]
    rows = x.shape[0]
    x = _ffn(x, w["norm_ffn1"], w["w_gu1"], w["w_d1"], rows)
    z, _ = _proj(x, w["norm_mix"], w["w_in"], tabs, w["gains"], rows, None, None)
    ka, va, kd, vd = (_cols(z, n)[:nb] for n in ("ka", "va", "kd", "vd"))
    kw = _cols(z, "kw")[:nb]
    ki = kw[:, :IDX_DIM]
    wi = kw[:, IDX_DIM:IDX_DIM + IDX_HEADS]
    qa = _cols(z, "qa")[:nb].reshape(nb, DA_KV_HEADS, 2, 2, DA_QK_DIM)
    blk = jnp.einsum("bkgmd,mn->bkgmnd", qa * DA_QK_DIM ** -0.5, jnp.eye(2, dtype=F32))
    wq_da = _expand_queries(blk.reshape(nb, DA_KV_HEADS, 4, 2 * DA_QK_DIM), DA_KV_HEADS, 4, LANES, 1.0)
    oa = _paged_attn("da", layer, page_table, wq_da, da_kt, da_v,
                     ka.reshape(nb, 1, -1), va.reshape(nb, 1, -1), w["lp"], lam_init)
    qi = _cols(z, "qi")[:nb]
    scores = _paged_index(layer, page_table, qi.reshape(nb, IDX_HEADS, IDX_DIM).astype(BF16),
                          wi.reshape(nb, IDX_HEADS, 1), idx_kt)
    bias = _sample_select(scores.reshape(nb, -1), qi, jnp.tile(ki, (1, IDX_HEADS)),
                          jnp.pad(wi, ((0, 0), (0, LANES - IDX_HEADS))), topk)
    qd = _cols(z, "qd")[:nb].reshape(nb, DSA_KV_HEADS, 2, DSA_HEAD_DIM)
    wq_dsa = _expand_queries(qd, DSA_KV_HEADS, 2, DSA_HEAD_DIM, DSA_HEAD_DIM ** -0.5)
    od = _paged_attn("dsa", layer, page_table, wq_dsa, dsa_k, dsa_v,
                     kd.reshape(nb, 1, -1), vd.reshape(nb, 1, -1), bias.reshape(nb, 1, -1), None)
    pad = ((0, rows - nb), (0, 0))
    x = _merge(x, jnp.pad(oa.reshape(nb, -1), pad), jnp.pad(od.reshape(nb, -1), pad), z,
               w["out_norm"], w["wa"], w["wb"], w["wo"], lam_init, rows)
    x = _ffn(x, w["norm_ffn2"], w["w_gu2"], w["w_d2"], rows)
    return x, (ka, va, kd, vd, ki)


def kernel(x_prompt, x_sample, cache_da_k, cache_da_v, cache_dsa_k, cache_dsa_v, cache_idx_k, page_table,
           meta_tokens, norm_ffn1, ffn1_w_gu, ffn1_w_down, norm_mix, w_in, da_q_norm, da_k_norm,
           dsa_q_norm, dsa_k_norm, da_lambda, da_out_norm, w_branch_a, w_branch_b, w_out,
           norm_ffn2, ffn2_w_gu, ffn2_w_down):
    params = dict(norm_ffn1=norm_ffn1, ffn1_w_gu=ffn1_w_gu, ffn1_w_down=ffn1_w_down, norm_mix=norm_mix,
                  w_in=w_in, da_q_norm=da_q_norm, da_k_norm=da_k_norm, dsa_q_norm=dsa_q_norm,
                  dsa_k_norm=dsa_k_norm, da_lambda=da_lambda, da_out_norm=da_out_norm,
                  w_branch_a=w_branch_a, w_branch_b=w_branch_b, w_out=w_out, norm_ffn2=norm_ffn2,
                  ffn2_w_gu=ffn2_w_gu, ffn2_w_down=ffn2_w_down)
    depth, d_model = norm_mix.shape
    assert w_in.shape[2] == sum(SPLITS) + 2 * d_model and d_model == DA_HEADS * DA_V_DIM
    assert x_prompt.shape[0] == 1 and x_sample.shape[1] == 1
    weights = [_layer_weights(l, params) for l in range(depth)]
    lam_inits = [0.8 - 0.6 * math.exp(-0.3 * l) for l in range(depth)]

    seq = x_prompt.shape[1]
    t = seq + N_META
    t_pad = -(-t // T_ALIGN) * T_ALIGN
    xp = jnp.concatenate([meta_tokens.astype(F32), x_prompt[0], jnp.zeros((t_pad - t, d_model), F32)], axis=0)
    pos_p = jnp.arange(t_pad, dtype=jnp.int32)
    tabs_p = _rope_tables(pos_p, DA_QK_DIM) + _rope_tables(pos_p, DSA_HEAD_DIM)
    topk_p = min(TOPK_MAX, t // 4)
    rows_p = []
    for l in range(depth):
        xp, rows = _prompt_layer(xp, weights[l], tabs_p, lam_inits[l], topk_p, t)
        rows_p.append(tuple(r[:t] for r in rows))
    y_prompt = xp[N_META:t][None]

    nb, n_pages = page_table.shape
    past = n_pages * PAGE_SIZE
    rows_s_pad = -(-nb // 16) * 16
    xs = jnp.pad(x_sample[:, 0, :], ((0, rows_s_pad - nb), (0, 0)))
    pos_s = jnp.full((rows_s_pad,), past, jnp.int32)
    tabs_s = _rope_tables(pos_s, DA_QK_DIM) + _rope_tables(pos_s, DSA_HEAD_DIM)
    topk_s = min(TOPK_MAX, (past + 1) // 4)
    rows_s = []
    caches = _cache_views(cache_da_k, cache_da_v, cache_dsa_k, cache_dsa_v, cache_idx_k)
    for l in range(depth):
        xs, rows = _sample_layer(xs, weights[l], tabs_s, lam_inits[l], topk_s, l, caches, page_table)
        rows_s.append(rows)
    y_sample = xs[:nb][:, None, :]

    def stack(rows, i, shape):
        return jnp.stack([r[i] for r in rows], axis=0).reshape((depth,) + shape)

    lead_p, lead_s = (1, t), (nb, 1)
    return (y_prompt, y_sample,
            stack(rows_p, 0, lead_p + (DA_KV_HEADS, 2, DA_QK_DIM)), stack(rows_p, 1, lead_p + (DA_KV_HEADS, DA_V_DIM)),
            stack(rows_p, 2, lead_p + (DSA_KV_HEADS, DSA_HEAD_DIM)), stack(rows_p, 3, lead_p + (DSA_KV_HEADS, DSA_HEAD_DIM)),
            stack(rows_p, 4, lead_p + (IDX_DIM,)),
            stack(rows_s, 0, lead_s + (DA_KV_HEADS, 2, DA_QK_DIM)), stack(rows_s, 1, lead_s + (DA_KV_HEADS, DA_V_DIM)),
            stack(rows_s, 2, lead_s + (DSA_KV_HEADS, DSA_HEAD_DIM)), stack(rows_s, 3, lead_s + (DSA_KV_HEADS, DSA_HEAD_DIM)),
            stack(rows_s, 4, lead_s + (IDX_DIM,)))
```

```python
import functools
import math

import jax
import jax.numpy as jnp
import numpy as np
from jax import lax
from jax.experimental import pallas as pl
from jax.experimental.pallas import tpu as pltpu

F32 = jnp.float32
BF16 = jnp.bfloat16
I32 = jnp.int32

N_META = 16
ROPE_THETA = 500000.0
EPS = 1e-6
PAGE_SIZE = 128
DA_HEADS, DA_KV_HEADS, DA_QK_DIM, DA_V_DIM = 8, 4, 64, 128
DSA_HEADS, DSA_KV_HEADS, DSA_HEAD_DIM = 8, 4, 128
IDX_HEADS, IDX_DIM = 8, 64
TOPK_MAX = 256
SPLITS = (DA_HEADS * 2 * DA_QK_DIM, DA_KV_HEADS * 2 * DA_QK_DIM, DA_KV_HEADS * DA_V_DIM,
          DSA_HEADS * DSA_HEAD_DIM, DSA_KV_HEADS * DSA_HEAD_DIM, DSA_KV_HEADS * DSA_HEAD_DIM,
          IDX_HEADS * IDX_DIM, IDX_DIM, IDX_HEADS)

LANES = 128
MXU_COLS = 256
VMEM_LIMIT_BYTES = 56 * 1024 * 1024

PROJ_TN = 2 * MXU_COLS
_OFF = {}
_o = 0
for _name, _w in (("qa", 1024), ("ka", 512), ("va", 512), ("qd", 1024), ("kd", 512), ("vd", 512),
                  ("ga", 1024), ("gb", 1024), ("qi", 512), ("kw", PROJ_TN)):
    _OFF[_name] = (_o, _w)
    _o += _w
PROJ_N = _o

LOG2E = math.log2(math.e)
NEG = -1e30
INT_MIN = -2 ** 31

TQ = 512
TK = 512
TQ_SEL = 256
SHORTLIST = 12
T_ALIGN = 1536
TM_FFN = 768
TM_PROJ = 768
TM_MERGE = 512
FF_CHUNK = 256
PAGES_PER_STEP = 16
INDEX_PAGES_PER_STEP = 32


def _cparams(*sem):
    return pltpu.CompilerParams(dimension_semantics=sem, vmem_limit_bytes=VMEM_LIMIT_BYTES)


def _ffn_kernel(x_ref, g_ref, wg_ref, wu_ref, wd_ref, o_ref, h_scr):
    c = pl.program_id(1)

    @pl.when(c == 0)
    def _():
        x = x_ref[...]
        ms = jnp.mean(x * x, axis=-1, keepdims=True)
        h_scr[...] = ((x * lax.rsqrt(ms + EPS)) * g_ref[...]).astype(BF16)
        o_ref[...] = x

    h = h_scr[...]
    g = jnp.dot(h, wg_ref[...], preferred_element_type=F32)
    u = jnp.dot(h, wu_ref[...], preferred_element_type=F32)
    a = ((g * jax.nn.sigmoid(g)) * u).astype(BF16)
    o_ref[...] += 0.5 * jnp.dot(a, wd_ref[...], preferred_element_type=F32)


def _ffn(x, gain, w_gu, w_down, tm):
    m, d = x.shape
    d_ff = w_down.shape[0]
    n_c = d_ff // FF_CHUNK
    return pl.pallas_call(
        _ffn_kernel,
        out_shape=jax.ShapeDtypeStruct((m, d), F32),
        grid=(m // tm, n_c),
        in_specs=[
            pl.BlockSpec((tm, d), lambda i, c: (i, 0)),
            pl.BlockSpec((1, d), lambda i, c: (0, 0)),
            pl.BlockSpec((d, FF_CHUNK), lambda i, c: (0, c)),
            pl.BlockSpec((d, FF_CHUNK), lambda i, c: (0, c + n_c)),
            pl.BlockSpec((FF_CHUNK, d), lambda i, c: (c, 0)),
        ],
        out_specs=pl.BlockSpec((tm, d), lambda i, c: (i, 0)),
        scratch_shapes=[pltpu.VMEM((tm, d), BF16)],
        compiler_params=_cparams("parallel", "arbitrary"),
        name="ffn",
    )(x, gain.reshape(1, d), w_gu, w_gu, w_down)


def _seg_rms(x, gain, seg):
    cols = []
    for b in range(x.shape[1] // LANES):
        blk = x[:, b * LANES:(b + 1) * LANES]
        sq = blk * blk
        if seg == LANES:
            ms = jnp.sum(sq, axis=-1, keepdims=True) * (1.0 / seg)
        else:
            lane = lax.broadcasted_iota(I32, blk.shape, 1)
            lo = jnp.sum(jnp.where(lane < seg, sq, 0.0), axis=-1, keepdims=True)
            hi = jnp.sum(jnp.where(lane >= seg, sq, 0.0), axis=-1, keepdims=True)
            ms = jnp.where(lane < seg, lo, hi) * (1.0 / seg)
        cols.append((blk * lax.rsqrt(ms + EPS)) * gain)
    return jnp.concatenate(cols, axis=1)


def _rope(x, cos, sin, half):
    n = x.shape[1]
    reps = n // LANES
    c = jnp.concatenate([cos] * reps, axis=1)
    s = jnp.concatenate([sin] * reps, axis=1)
    up = pltpu.roll(x, n - half, 1)
    dn = pltpu.roll(x, half, 1)
    lane = lax.broadcasted_iota(I32, x.shape, 1)
    partner = jnp.where((lane & (2 * half - 1)) < half, up, dn)
    return x * c + partner * s


def _proj_kernel(qa_scale, qd_scale, x_ref, g_ref, w_ref, c64_ref, s64_ref, c128_ref, s128_ref,
                 qa_g, ka_g, qd_g, kd_g, o_ref, ob_ref, h_scr):
    j = pl.program_id(1)

    def emit(r, scale=None):
        o_ref[...] = r
        ob_ref[...] = (r if scale is None else r * scale).astype(BF16)

    @pl.when(j == 0)
    def _():
        x = x_ref[...]
        ms = jnp.mean(x * x, axis=-1, keepdims=True)
        h_scr[...] = ((x * lax.rsqrt(ms + EPS)) * g_ref[...]).astype(BF16)

    z = jnp.dot(h_scr[...], w_ref[...], preferred_element_type=F32)

    def tile_of(name):
        off, width = _OFF[name]
        return off // PROJ_TN, (off + width) // PROJ_TN

    def in_group(name):
        lo, hi = tile_of(name)
        return jnp.logical_and(j >= lo, j < hi)

    h64, h128 = DA_QK_DIM // 8, DSA_HEAD_DIM // 8

    @pl.when(in_group("qa"))
    def _():
        emit(_rope(_seg_rms(z, qa_g[...], DA_QK_DIM), c64_ref[...], s64_ref[...], h64), qa_scale)

    @pl.when(in_group("ka"))
    def _():
        emit(_rope(_seg_rms(z, ka_g[...], DA_QK_DIM), c64_ref[...], s64_ref[...], h64))

    @pl.when(in_group("qd"))
    def _():
        emit(_rope(_seg_rms(z, qd_g[...], DSA_HEAD_DIM), c128_ref[...], s128_ref[...], h128), qd_scale)

    @pl.when(in_group("kd"))
    def _():
        emit(_rope(_seg_rms(z, kd_g[...], DSA_HEAD_DIM), c128_ref[...], s128_ref[...], h128))

    @pl.when(jnp.logical_or(in_group("va"), in_group("vd")))
    def _():
        emit(z)

    @pl.when(in_group("qi"))
    def _():
        emit(_rope(z, c64_ref[...], s64_ref[...], h64))

    @pl.when(jnp.logical_or(in_group("ga"), in_group("gb")))
    def _():
        emit(jax.nn.sigmoid(z))

    @pl.when(in_group("kw"))
    def _():
        r = _rope(z, c64_ref[...], s64_ref[...], h64)
        lane = lax.broadcasted_iota(I32, z.shape, 1)
        wi_scale = IDX_HEADS ** -0.5 * IDX_DIM ** -0.5
        emit(jnp.where(lane < IDX_DIM, r, z * wi_scale))


def _proj(x, gain, w_packed, tabs, gains, tm, qa_scale, qd_scale):
    m, d = x.shape
    n = w_packed.shape[1]
    c64, s64, c128, s128 = tabs
    row = lambda i, j: (i, 0)
    const = lambda i, j: (0, 0)
    return pl.pallas_call(
        functools.partial(_proj_kernel, qa_scale, qd_scale),
        out_shape=(jax.ShapeDtypeStruct((m, n), F32), jax.ShapeDtypeStruct((m, n), BF16)),
        grid=(m // tm, n // PROJ_TN),
        in_specs=[
            pl.BlockSpec((tm, d), row),
            pl.BlockSpec((1, d), const),
            pl.BlockSpec((d, PROJ_TN), lambda i, j: (0, j)),
            pl.BlockSpec((tm, LANES), row), pl.BlockSpec((tm, LANES), row),
            pl.BlockSpec((tm, LANES), row), pl.BlockSpec((tm, LANES), row),
            pl.BlockSpec((1, LANES), const), pl.BlockSpec((1, LANES), const),
            pl.BlockSpec((1, LANES), const), pl.BlockSpec((1, LANES), const),
        ],
        out_specs=(pl.BlockSpec((tm, PROJ_TN), lambda i, j: (i, j)),
                   pl.BlockSpec((tm, PROJ_TN), lambda i, j: (i, j))),
        scratch_shapes=[pltpu.VMEM((tm, d), BF16)],
        compiler_params=_cparams("parallel", "arbitrary"),
        name="proj",
    )(x, gain.reshape(1, d), w_packed, c64, s64, c128, s128, *gains)


def _merge_kernel(lam_init, x_ref, oa_ref, od_ref, ga_ref, gb_ref, g_ref, wa_ref, wb_ref, wo_ref,
                  o_ref, a_scr):
    for h in range(DA_HEADS):
        blk = oa_ref[:, h * DA_V_DIM:(h + 1) * DA_V_DIM]
        ms = jnp.mean(blk * blk, axis=-1, keepdims=True)
        y = ((blk * lax.rsqrt(ms + EPS)) * g_ref[...]) * (1.0 - lam_init)
        a_scr[:, h * DA_V_DIM:(h + 1) * DA_V_DIM] = y.astype(BF16)
    a = jnp.dot(a_scr[...], wa_ref[...], preferred_element_type=F32)
    b = jnp.dot(od_ref[...].astype(BF16), wb_ref[...], preferred_element_type=F32)
    merged = ga_ref[...] * a + gb_ref[...] * b
    o_ref[...] = x_ref[...] + jnp.dot(merged.astype(BF16), wo_ref[...], preferred_element_type=F32)


def _merge(x, oa, od, z, gain, wa, wb, wo, lam_init, tm):
    m, d = x.shape
    row = lambda i: (i, 0)
    const = lambda i: (0, 0)
    ga_blk = _OFF["ga"][0] // d
    gb_blk = _OFF["gb"][0] // d
    return pl.pallas_call(
        functools.partial(_merge_kernel, lam_init),
        out_shape=jax.ShapeDtypeStruct((m, d), F32),
        grid=(m // tm,),
        in_specs=[
            pl.BlockSpec((tm, d), row), pl.BlockSpec((tm, d), row), pl.BlockSpec((tm, d), row),
            pl.BlockSpec((tm, d), lambda i: (i, ga_blk)), pl.BlockSpec((tm, d), lambda i: (i, gb_blk)),
            pl.BlockSpec((1, DA_V_DIM), const),
            pl.BlockSpec((d, d), const), pl.BlockSpec((d, d), const), pl.BlockSpec((d, d), const),
        ],
        out_specs=pl.BlockSpec((tm, d), row),
        scratch_shapes=[pltpu.VMEM((tm, d), BF16)],
        compiler_params=_cparams("parallel"),
        name="merge",
    )(x, oa, od, z, z, gain.reshape(1, DA_V_DIM), wa, wb, wo)


def _lambda_of(lp, lam_init):
    a = jnp.sum(lp[0:1, :] * lp[1:2, :], axis=-1, keepdims=True)
    b = jnp.sum(lp[2:3, :] * lp[3:4, :], axis=-1, keepdims=True)
    return jnp.exp(a) - jnp.exp(b) + lam_init


def _softmax_step(s, v, m_scr, l_scr, acc_scr):
    m_prev = m_scr[...]
    m_new = jnp.maximum(m_prev, jnp.max(s, axis=-1, keepdims=True))
    alpha = jnp.exp(m_prev - m_new)
    p = jnp.exp(s - m_new)
    l_scr[...] = alpha * l_scr[...] + jnp.sum(p, axis=-1, keepdims=True)
    acc_scr[...] = alpha * acc_scr[...] + jnp.dot(p.astype(BF16), v, preferred_element_type=F32)
    m_scr[...] = m_new


def _flash_step(s, v_ext, m_ref, acc_ref):
    m_prev = m_ref[...]
    m_new = jnp.maximum(m_prev, jnp.max(s, axis=-1, keepdims=True))
    alpha = jnp.exp2(m_prev - m_new)
    p = jnp.exp2(s - jnp.concatenate([m_new] * (s.shape[1] // LANES), axis=1))
    acc_ref[...] = (jnp.concatenate([alpha] * (acc_ref.shape[1] // LANES), axis=1) * acc_ref[...]
                    + jnp.dot(p.astype(BF16), v_ext, preferred_element_type=F32))
    m_ref[...] = m_new


def _diff_attn_kernel(lam_init, tail, q_ref, kt_ref, v_ref, lp_ref, o_ref, lhs_scr, m_scr, acc_scr):
    qt = pl.program_id(1)

    def run(tq):
        rows = 4 * tq
        q = q_ref[0:tq, :]
        lane = lax.broadcasted_iota(I32, (tq, LANES), 1)
        for g in range(2):
            qg = q[:, g * LANES:(g + 1) * LANES]
            for mp in range(2):
                keep = (lane < DA_QK_DIM) if mp == 0 else (lane >= DA_QK_DIM)
                r = (g * 2 + mp) * tq
                lhs_scr[r:r + tq, :] = jnp.where(keep, qg, jnp.zeros_like(qg))
        m_ref, acc_ref = m_scr.at[0:rows], acc_scr.at[0:rows]
        m_ref[...] = jnp.full((rows, LANES), NEG, F32)
        acc_ref[...] = jnp.zeros((rows, 2 * DA_V_DIM), F32)

        def step(kt, masked):
            s = jnp.dot(lhs_scr[0:rows, :], kt_ref[0, kt], preferred_element_type=F32)
            if masked:
                qpos = qt * TQ + (lax.broadcasted_iota(I32, (rows, TK), 0) & (tq - 1))
                col = lax.broadcasted_iota(I32, (rows, TK), 1) + kt * TK
                s = jnp.where(col <= qpos, s, NEG)
            v = v_ref[pl.ds(pl.multiple_of(kt * TK, TK), TK), :]
            _flash_step(s, v, m_ref, acc_ref)

        def body(kt, carry):
            step(kt, False)
            return carry

        lax.fori_loop(0, qt, body, 0)
        step(qt, True)
        acc = acc_ref[...]
        o = acc[:, :DA_V_DIM] / acc[:, DA_V_DIM:]
        lam = _lambda_of(lp_ref[...], lam_init)
        for g in range(2):
            r = g * 2 * tq
            o_ref[0:tq, g * DA_V_DIM:(g + 1) * DA_V_DIM] = o[r:r + tq, :] - lam * o[r + tq:r + 2 * tq, :]
        if tq < TQ:
            o_ref[tq:TQ, :] = jnp.zeros((TQ - tq, o_ref.shape[1]), F32)

    if tail is None:
        run(TQ)
    else:
        pl.when(qt != tail[0])(lambda: run(TQ))
        pl.when(qt == tail[0])(lambda: run(tail[1]))


def _tail_tile(t_real, t_pad):
    rem = t_real % TQ
    if t_pad - t_real >= TQ or rem == 0 or rem > TQ // 4:
        return None
    return t_real // TQ, max(16, pl.next_power_of_2(rem))


def _diff_attn(q, kt, v_ext, lp, lam_init, t_real):
    t = q.shape[0]
    n_kt = t // TK
    assert TQ == TK
    return pl.pallas_call(
        functools.partial(_diff_attn_kernel, lam_init, _tail_tile(t_real, t)),
        out_shape=jax.ShapeDtypeStruct((t, DA_HEADS * DA_V_DIM), F32),
        grid=(DA_KV_HEADS, t // TQ),
        in_specs=[
            pl.BlockSpec((TQ, 2 * LANES), lambda kv, i: (i, kv)),
            pl.BlockSpec((1, n_kt, LANES, TK), lambda kv, i: (kv, 0, 0, 0)),
            pl.BlockSpec((t, 2 * DA_V_DIM), lambda kv, i: (0, kv)),
            pl.BlockSpec((4, DA_QK_DIM), lambda kv, i: (0, 0)),
        ],
        out_specs=pl.BlockSpec((TQ, 2 * DA_V_DIM), lambda kv, i: (i, kv)),
        scratch_shapes=[pltpu.VMEM((4 * TQ, LANES), BF16), pltpu.VMEM((4 * TQ, LANES), F32),
                        pltpu.VMEM((4 * TQ, 2 * DA_V_DIM), F32)],
        compiler_params=_cparams("parallel", "arbitrary"),
        name="diff_attn",
    )(q, kt, v_ext, lp)


def _float_key(s):
    s = jnp.where(s == 0.0, 0.0, s)
    bits = pltpu.bitcast(s, I32)
    return jnp.where(bits < 0, bits ^ jnp.int32(0x7FFFFFFF), bits)


def _key_float(key):
    bits = jnp.where(key < 0, key ^ jnp.int32(0x7FFFFFFF), key)
    return pltpu.bitcast(bits, F32)


KEY_LOWEST_FINITE = INT_MIN + 0x00800000


def _kth_largest_key(count_ge, rows, k):
    def bit_body(i, t):
        cand = t + lax.shift_left(jnp.int32(1), 31 - i)
        return jnp.where(count_ge(cand) >= k, cand, t)

    return lax.fori_loop(0, 32, bit_body, jnp.full((rows, 1), INT_MIN, I32))


def _select_kernel(topk, t_real, qi_ref, wi_ref, ki_ref, o_ref, lhs_scr, sc_scr, short_scr, t_scr, n_scr):
    qt = pl.program_id(0)
    n_kt = sc_scr.shape[0]
    n_rows = jnp.where(qt * TQ_SEL < t_real, jnp.minimum((qt + 1) * TQ_SEL, t_real), 1)
    n_k = (n_rows + TK - 1) // TK
    lane = lax.broadcasted_iota(I32, (TQ_SEL, LANES), 1)
    for h in range(IDX_HEADS):
        pair = qi_ref[:, (h // 2) * LANES:(h // 2 + 1) * LANES]
        keep = (lane < IDX_DIM) if h % 2 == 0 else (lane >= IDX_DIM)
        lhs_scr[h * TQ_SEL:(h + 1) * TQ_SEL, :] = jnp.where(keep, pair, jnp.zeros_like(pair))
    wi = wi_ref[...]
    w_cols = [wi[:, h:h + 1] for h in range(IDX_HEADS)]
    qpos = qt * TQ_SEL + lax.broadcasted_iota(I32, (TQ_SEL, TK), 0)
    qpos = jnp.where(qpos < t_real, qpos, 0)
    col = lax.broadcasted_iota(I32, (TQ_SEL, TK), 1)

    def score_body(kt, carry):
        d = jnp.dot(lhs_scr[...], ki_ref[kt], preferred_element_type=F32)
        sc = jnp.zeros((TQ_SEL, TK), F32)
        for h in range(IDX_HEADS):
            sc = sc + jnp.maximum(d[h * TQ_SEL:(h + 1) * TQ_SEL, :], 0.0) * w_cols[h]
        sc = jnp.where(sc == 0.0, 0.0, sc)
        sc_scr[kt] = jnp.where(col + kt * TK <= qpos, sc, -jnp.inf)
        return carry

    lax.fori_loop(0, n_k, score_body, 0)

    def count(pred):
        def body(kt, acc):
            hit = jnp.where(pred(sc_scr[kt], col + kt * TK), 1.0, 0.0)
            for c in range(TK // LANES):
                acc = acc + hit[:, c * LANES:(c + 1) * LANES]
            return acc

        acc = lax.fori_loop(0, n_k, body, jnp.zeros((TQ_SEL, LANES), F32))
        return jnp.sum(acc, axis=-1, keepdims=True)

    def shortlist_body(rg, carry):
        rows8 = [pl.ds(pl.multiple_of(rg * 16 + 8 * u, 8), 8) for u in range(2)]

        def insert_body(kt, tops):
            tops = list(tops)
            for c in range(TK // LANES):
                for u in range(2):
                    x = sc_scr[kt, rows8[u], c * LANES:(c + 1) * LANES]
                    for j in range(SHORTLIST):
                        top = tops[u * SHORTLIST + j]
                        tops[u * SHORTLIST + j] = jnp.maximum(top, x)
                        x = jnp.minimum(top, x)
            return tuple(tops)

        tops = lax.fori_loop(0, n_k, insert_body, (jnp.full((8, LANES), -jnp.inf, F32),) * (2 * SHORTLIST))
        for u in range(2):
            for j in range(SHORTLIST):
                short_scr[j, rows8[u], :] = _float_key(tops[u * SHORTLIST + j])
        return carry

    lax.fori_loop(0, TQ_SEL // 16, shortlist_body, 0)

    def count_short(cand):
        acc = jnp.zeros((TQ_SEL, LANES), F32)
        for j in range(SHORTLIST):
            acc = acc + jnp.where(short_scr[j] >= cand, 1.0, 0.0)
        return jnp.sum(acc, axis=-1, keepdims=True)

    kf = float(topk)
    t_key = jnp.maximum(_kth_largest_key(count_short, TQ_SEL, kf), KEY_LOWEST_FINITE)
    n_ge = count(lambda s, idx: s >= _key_float(t_key))
    t_scr[...] = t_key
    n_scr[...] = n_ge

    @pl.when(jnp.max(jnp.abs(n_ge - count_short(t_key))) > 0.0)
    def _():
        t_full = _kth_largest_key(lambda cand: count(lambda s, idx: s >= _key_float(cand)), TQ_SEL, kf)
        t_full = jnp.maximum(t_full, KEY_LOWEST_FINITE)
        t_scr[...] = t_full
        n_scr[...] = count(lambda s, idx: s >= _key_float(t_full))

    t = _key_float(t_scr[...])
    n_ge = n_scr[...]

    @pl.when(jnp.max(n_ge) > kf)
    def _():
        need = kf - count(lambda s, idx: s > t)

        def bit_body(i, x):
            cand = x + lax.shift_left(jnp.int32(1), 14 - i)
            below = count(lambda s, idx: jnp.logical_and(s == t, idx < cand))
            return jnp.where(below < need, cand, x)

        last = lax.fori_loop(0, 15, bit_body, jnp.zeros((TQ_SEL, 1), I32))

        def drop_body(kt, carry):
            s = sc_scr[kt]
            drop = jnp.logical_and(s == t, col + kt * TK > last)
            sc_scr[kt] = jnp.where(drop, -jnp.inf, s)
            return carry

        lax.fori_loop(0, n_k, drop_body, 0)

    def write_body(kt, carry):
        o_ref[kt] = jnp.where(sc_scr[kt] >= t, 0.0, NEG).astype(BF16)
        return carry

    lax.fori_loop(0, n_k, write_body, 0)

    def fill_body(kt, carry):
        o_ref[kt] = jnp.full((TQ_SEL, TK), NEG, BF16)
        return carry

    lax.fori_loop(n_k, n_kt, fill_body, 0)


def _select(qi, wi, ki2, topk, t_real):
    t = qi.shape[0]
    n_kt = t // TK
    assert t < 2 ** 15
    return pl.pallas_call(
        functools.partial(_select_kernel, topk, t_real),
        out_shape=jax.ShapeDtypeStruct((n_kt, t, TK), BF16),
        grid=(t // TQ_SEL,),
        in_specs=[
            pl.BlockSpec((TQ_SEL, IDX_HEADS * IDX_DIM), lambda i: (i, 0)),
            pl.BlockSpec((TQ_SEL, LANES), lambda i: (i, 0)),
            pl.BlockSpec((n_kt, LANES, TK), lambda i: (0, 0, 0)),
        ],
        out_specs=pl.BlockSpec((n_kt, TQ_SEL, TK), lambda i: (0, i, 0)),
        scratch_shapes=[pltpu.VMEM((IDX_HEADS * TQ_SEL, LANES), BF16),
                        pltpu.VMEM((n_kt, TQ_SEL, TK), F32),
                        pltpu.VMEM((SHORTLIST, TQ_SEL, LANES), I32),
                        pltpu.VMEM((TQ_SEL, 1), I32), pltpu.VMEM((TQ_SEL, 1), F32)],
        compiler_params=_cparams("parallel"),
        name="dsa_select",
    )(qi, wi, ki2)


def _dsa_attn_kernel(tail, qt_ref, kt_ref_s, q_ref, kt_ref, v_ref, b_ref, o_ref, lhs_scr, m_scr, acc_scr):
    step = pl.program_id(0)
    qt, kt = qt_ref[step], kt_ref_s[step]

    def run(tq):
        rows = 2 * tq

        @pl.when(kt == 0)
        def _():
            for kv in range(DSA_KV_HEADS):
                for g in range(2):
                    h = kv * 2 + g
                    lhs_scr[kv, g * tq:(g + 1) * tq, :] = q_ref[0:tq, h * LANES:(h + 1) * LANES]
            m_scr[:, 0:rows, :] = jnp.full((DSA_KV_HEADS, rows, LANES), NEG, F32)
            acc_scr[:, 0:rows, :] = jnp.zeros((DSA_KV_HEADS, rows, 2 * DSA_HEAD_DIM), F32)

        bias = b_ref[0, 0:tq, :].astype(F32)
        bias2 = jnp.concatenate([bias, bias], axis=0)
        for kv in range(DSA_KV_HEADS):
            s = jnp.dot(lhs_scr[kv, 0:rows, :], kt_ref[kv, 0], preferred_element_type=F32) + bias2
            v = v_ref[:, kv * 2 * DSA_HEAD_DIM:(kv + 1) * 2 * DSA_HEAD_DIM]
            _flash_step(s, v, m_scr.at[kv, 0:rows], acc_scr.at[kv, 0:rows])

        @pl.when(kt == qt)
        def _():
            for kv in range(DSA_KV_HEADS):
                acc = acc_scr[kv, 0:rows, :]
                o = acc[:, :DSA_HEAD_DIM] / acc[:, DSA_HEAD_DIM:]
                for g in range(2):
                    h = kv * 2 + g
                    o_ref[0:tq, h * DSA_HEAD_DIM:(h + 1) * DSA_HEAD_DIM] = o[g * tq:(g + 1) * tq, :]
            if tq < TQ:
                o_ref[tq:TQ, :] = jnp.zeros((TQ - tq, o_ref.shape[1]), F32)

    if tail is None:
        run(TQ)
    else:
        pl.when(qt != tail[0])(lambda: run(TQ))
        pl.when(qt == tail[0])(lambda: run(tail[1]))


def _dsa_attn(q, kt, v_ext, bias, t_real):
    t = q.shape[0]
    assert TQ == TK
    kv, d = DSA_KV_HEADS, DSA_HEAD_DIM
    n_t = t // TQ
    qt_of_step = np.concatenate([np.full(i + 1, i, np.int32) for i in range(n_t)])
    kt_of_step = np.concatenate([np.arange(i + 1, dtype=np.int32) for i in range(n_t)])
    grid_spec = pltpu.PrefetchScalarGridSpec(
        num_scalar_prefetch=2,
        grid=(qt_of_step.shape[0],),
        in_specs=[
            pl.BlockSpec((TQ, DSA_HEADS * d), lambda s, qt, kt: (qt[s], 0)),
            pl.BlockSpec((kv, 1, LANES, TK), lambda s, qt, kt: (0, kt[s], 0, 0)),
            pl.BlockSpec((TK, kv * 2 * d), lambda s, qt, kt: (kt[s], 0)),
            pl.BlockSpec((1, TQ, TK), lambda s, qt, kt: (kt[s], qt[s], 0)),
        ],
        out_specs=pl.BlockSpec((TQ, DSA_HEADS * d), lambda s, qt, kt: (qt[s], 0)),
        scratch_shapes=[pltpu.VMEM((kv, 2 * TQ, LANES), BF16), pltpu.VMEM((kv, 2 * TQ, LANES), F32),
                        pltpu.VMEM((kv, 2 * TQ, 2 * d), F32)],
    )
    return pl.pallas_call(
        functools.partial(_dsa_attn_kernel, _tail_tile(t_real, t)),
        out_shape=jax.ShapeDtypeStruct((t, DSA_HEADS * d), F32),
        grid_spec=grid_spec,
        compiler_params=_cparams("arbitrary"),
        name="dsa_attn",
    )(jnp.asarray(qt_of_step), jnp.asarray(kt_of_step), q, kt, v_ext, bias)


STREAM_ROWS = 16


def _paged_attn_kernel(mode, lam_init, n_pages, npg, *refs):
    pt_ref, wq_ref = refs[0], refs[1]
    k_refs = refs[2:2 + npg]
    v_refs = refs[2 + npg:2 + 2 * npg]
    knew_ref, vnew_ref = refs[2 + 2 * npg], refs[3 + 2 * npg]
    pos = 4 + 2 * npg
    if mode == "da":
        lp_ref = refs[pos]
        pos += 1
    else:
        bias_ref, bnew_ref = refs[pos], refs[pos + 1]
        pos += 2
    o_ref, m_scr, l_scr, acc_scr = refs[pos:pos + 4]
    j = pl.program_id(1)
    wq = wq_ref[0]

    @pl.when(j == 0)
    def _():
        s_new = jnp.sum(wq.astype(F32) * knew_ref[0], axis=-1, keepdims=True)
        if mode == "dsa":
            s_new = s_new + bnew_ref[0][:, 0:1].astype(F32)
        m_scr[...] = s_new
        l_scr[...] = jnp.ones(l_scr.shape, F32)
        acc_scr[...] = jnp.broadcast_to(vnew_ref[0], acc_scr.shape)

    def token_rows(ref):
        n_kv = ref.shape[2] // PAGE_SIZE
        return jnp.concatenate([ref[0, 0, pl.ds(kv, PAGE_SIZE, stride=n_kv), :] for kv in range(n_kv)], axis=1)

    v = jnp.concatenate([token_rows(r) for r in v_refs], axis=0).astype(BF16)
    if mode == "da":
        kt = jnp.concatenate([r[0, 0].reshape(-1, PAGE_SIZE) for r in k_refs], axis=1).astype(BF16)
        s = jnp.dot(wq, kt, preferred_element_type=F32)
    else:
        k = jnp.concatenate([token_rows(r) for r in k_refs], axis=0).astype(BF16)
        s = lax.dot_general(wq, k, (((1,), (1,)), ((), ())), preferred_element_type=F32)
        s = s + bias_ref[0].astype(F32)
    _softmax_step(s, v, m_scr, l_scr, acc_scr)

    @pl.when(j == n_pages // npg - 1)
    def _():
        o = acc_scr[...] / l_scr[...]
        if mode == "da":
            lam = _lambda_of(lp_ref[...], lam_init)
            for kv in range(DA_KV_HEADS):
                for g in range(2):
                    r = kv * 4 + g * 2
                    blk = slice(kv * DA_V_DIM, (kv + 1) * DA_V_DIM)
                    h = kv * 2 + g
                    o_ref[0, :, h * DA_V_DIM:(h + 1) * DA_V_DIM] = o[r:r + 1, blk] - lam * o[r + 1:r + 2, blk]
        else:
            for kv in range(DSA_KV_HEADS):
                for g in range(2):
                    r = kv * 2 + g
                    o_ref[0, :, r * DSA_HEAD_DIM:(r + 1) * DSA_HEAD_DIM] = (
                        o[r:r + 1, kv * DSA_HEAD_DIM:(kv + 1) * DSA_HEAD_DIM])


def _paged_attn(mode, layer, page_table, wq, k_cache, v_cache, k_new, v_new, extra, lam_init):
    nb, n_pages = page_table.shape
    npg = math.gcd(PAGES_PER_STEP, n_pages)
    width = k_new.shape[2]

    def page_spec(cache, i):
        tail = cache.shape[2:]
        zeros = (0,) * len(tail)
        return pl.BlockSpec((1, 1) + tail, lambda b, j, pt: (layer, pt[b * n_pages + j * npg + i]) + zeros)

    per_seq = lambda b, j, pt: (b, 0, 0)
    in_specs = [pl.BlockSpec((1, STREAM_ROWS, width), per_seq)]
    in_specs += [page_spec(k_cache, i) for i in range(npg)] + [page_spec(v_cache, i) for i in range(npg)]
    in_specs += [pl.BlockSpec((1, 1, width), per_seq), pl.BlockSpec((1, 1, width), per_seq)]
    args = [wq] + [k_cache] * npg + [v_cache] * npg + [k_new, v_new]
    if mode == "da":
        in_specs.append(pl.BlockSpec((4, DA_QK_DIM), lambda b, j, pt: (0, 0)))
        args.append(extra)
    else:
        in_specs.append(pl.BlockSpec((1, 1, npg * PAGE_SIZE), lambda b, j, pt: (b, 0, j)))
        in_specs.append(pl.BlockSpec((1, 1, PAGE_SIZE), lambda b, j, pt: (b, 0, n_pages)))
        args += [extra, extra]
    grid_spec = pltpu.PrefetchScalarGridSpec(
        num_scalar_prefetch=1,
        grid=(nb, n_pages // npg),
        in_specs=in_specs,
        out_specs=pl.BlockSpec((1, 1, 8 * LANES), per_seq),
        scratch_shapes=[pltpu.VMEM((STREAM_ROWS, 1), F32), pltpu.VMEM((STREAM_ROWS, 1), F32),
                        pltpu.VMEM((STREAM_ROWS, width), F32)],
    )
    return pl.pallas_call(
        functools.partial(_paged_attn_kernel, mode, lam_init, n_pages, npg),
        out_shape=jax.ShapeDtypeStruct((nb, 1, 8 * LANES), F32),
        grid_spec=grid_spec,
        compiler_params=_cparams("parallel", "arbitrary"),
        name="paged_" + mode,
    )(page_table.reshape(-1), *args)


def _paged_index_kernel(*refs):
    npg = len(refs) - 4
    q_ref, w_ref = refs[1], refs[2]
    k_refs = refs[3:3 + npg]
    o_ref = refs[3 + npg]
    kt = jnp.concatenate([r[0, 0] for r in k_refs], axis=1).astype(BF16)
    d = jnp.dot(q_ref[0], kt, preferred_element_type=F32)
    o_ref[0] = jnp.sum(jnp.maximum(d, 0.0) * w_ref[0], axis=0, keepdims=True)


def _paged_index(layer, page_table, qi, wi, k_cache):
    nb, n_pages = page_table.shape
    npg = math.gcd(INDEX_PAGES_PER_STEP, n_pages)

    def page_spec(i):
        return pl.BlockSpec((1, 1, IDX_DIM, PAGE_SIZE),
                            lambda b, j, pt: (layer, pt[b * n_pages + j * npg + i], 0, 0))

    per_seq = lambda b, j, pt: (b, 0, 0)
    grid_spec = pltpu.PrefetchScalarGridSpec(
        num_scalar_prefetch=1,
        grid=(nb, n_pages // npg),
        in_specs=[pl.BlockSpec((1, IDX_HEADS, IDX_DIM), per_seq), pl.BlockSpec((1, IDX_HEADS, 1), per_seq)]
        + [page_spec(i) for i in range(npg)],
        out_specs=pl.BlockSpec((1, 1, npg * PAGE_SIZE), lambda b, j, pt: (b, 0, j)),
    )
    return pl.pallas_call(
        _paged_index_kernel,
        out_shape=jax.ShapeDtypeStruct((nb, 1, n_pages * PAGE_SIZE), F32),
        grid_spec=grid_spec,
        compiler_params=_cparams("parallel", "arbitrary"),
        name="paged_index",
    )(page_table.reshape(-1), qi, wi, *([k_cache] * npg))


def _sample_select_kernel(topk, sc_ref, qi_ref, kin_ref, wi_ref, o_ref, key_scr):
    nb, past = sc_ref.shape
    width = past + LANES
    prod = qi_ref[...] * kin_ref[...]
    lane = lax.broadcasted_iota(I32, (nb, LANES), 1)
    wi = wi_ref[...]
    s_new = jnp.zeros((nb, 1), F32)
    for h in range(IDX_HEADS):
        blk = prod[:, (h // 2) * LANES:(h // 2 + 1) * LANES]
        keep = (lane < IDX_DIM) if h % 2 == 0 else (lane >= IDX_DIM)
        d = jnp.sum(jnp.where(keep, blk, 0.0), axis=-1, keepdims=True)
        s_new = s_new + jnp.maximum(d, 0.0) * wi[:, h:h + 1]
    key_scr[:, 0:past] = _float_key(sc_ref[...])
    key_scr[:, past:width] = jnp.where(lane == 0, _float_key(s_new), INT_MIN)
    col = lax.broadcasted_iota(I32, (nb, width), 1)

    def count(pred):
        return jnp.sum(jnp.where(pred(key_scr[...], col), 1.0, 0.0), axis=-1, keepdims=True)

    kf = float(topk)
    t = _kth_largest_key(lambda cand: count(lambda keys, idx: keys >= cand), nb, kf)
    t = jnp.maximum(t, INT_MIN + 1)
    n_ge = count(lambda keys, idx: keys >= t)

    @pl.when(jnp.max(n_ge) > kf)
    def _():
        need = kf - count(lambda keys, idx: keys > t)

        def bit_body(i, x):
            cand = x + lax.shift_left(jnp.int32(1), 14 - i)
            below = count(lambda keys, idx: jnp.logical_and(keys == t, idx < cand))
            return jnp.where(below < need, cand, x)

        last = lax.fori_loop(0, 15, bit_body, jnp.zeros((nb, 1), I32))
        keys = key_scr[...]
        key_scr[...] = jnp.where(jnp.logical_and(keys == t, col > last), INT_MIN, keys)

    o_ref[...] = jnp.where(key_scr[...] >= t, 0.0, NEG).astype(BF16)


def _sample_select(scores, qi, ki_new8, wi, topk):
    nb, past = scores.shape
    assert past + LANES < 2 ** 15
    return pl.pallas_call(
        functools.partial(_sample_select_kernel, topk),
        out_shape=jax.ShapeDtypeStruct((nb, past + LANES), BF16),
        scratch_shapes=[pltpu.VMEM((nb, past + LANES), I32)],
        compiler_params=pltpu.CompilerParams(vmem_limit_bytes=VMEM_LIMIT_BYTES),
        name="sample_select",
    )(scores, qi, ki_new8, wi)


def _rope_tables(pos, d):
    r = d // 4
    half = r // 2
    inv = 1.0 / (ROPE_THETA ** (jnp.arange(half, dtype=F32) * (2.0 / r)))
    ang = pos.astype(F32)[:, None] * inv[None, :]
    cos, sin = jnp.cos(ang), jnp.sin(ang)
    n = pos.shape[0]
    c = jnp.concatenate([cos, cos, jnp.ones((n, d - r), F32)], axis=1)
    s = jnp.concatenate([-sin, sin, jnp.zeros((n, d - r), F32)], axis=1)
    reps = LANES // d
    return jnp.tile(c, (1, reps)), jnp.tile(s, (1, reps))


def _pack_w_in(w):
    parts = jnp.split(w, np.cumsum(SPLITS).tolist(), axis=1)
    qa, ka, va, qd, kd, vd, qi, ki, wi, gates = parts
    ga, gb = jnp.split(gates, 2, axis=1)
    pad = jnp.zeros((w.shape[0], PROJ_TN - IDX_DIM - IDX_HEADS), w.dtype)
    return jnp.concatenate([qa, ka, va, qd, kd, vd, ga, gb, qi, ki, wi, pad], axis=1).astype(BF16)


def _cols(z, name):
    off, width = _OFF[name]
    return z[:, off:off + width]


def _key_tiles(k, n_heads):
    t = k.shape[0]
    kt = k.reshape(t // TK, TK, n_heads, LANES)
    return jnp.transpose(kt, (2, 0, 3, 1))


def _with_ones(v, n_heads):
    t = v.shape[0]
    v3 = v.reshape(t, n_heads, LANES)
    return jnp.concatenate([v3, jnp.ones_like(v3)], axis=2).reshape(t, n_heads * 2 * LANES)


def _layer_weights(l, p):
    return dict(
        norm_ffn1=p["norm_ffn1"][l], w_gu1=p["ffn1_w_gu"][l].astype(BF16), w_d1=p["ffn1_w_down"][l].astype(BF16),
        norm_mix=p["norm_mix"][l], w_in=_pack_w_in(p["w_in"][l]),
        gains=tuple(jnp.tile(p[n][l], LANES // p[n].shape[1]).reshape(1, LANES)
                    for n in ("da_q_norm", "da_k_norm", "dsa_q_norm", "dsa_k_norm")),
        lp=p["da_lambda"][l], out_norm=p["da_out_norm"][l],
        wa=p["w_branch_a"][l].astype(BF16), wb=p["w_branch_b"][l].astype(BF16), wo=p["w_out"][l].astype(BF16),
        norm_ffn2=p["norm_ffn2"][l], w_gu2=p["ffn2_w_gu"][l].astype(BF16), w_d2=p["ffn2_w_down"][l].astype(BF16),
    )


def _prompt_layer(x, w, tabs, lam_init, topk, t_real):
    x = _ffn(x, w["norm_ffn1"], w["w_gu1"], w["w_d1"], TM_FFN)
    z, zb = _proj(x, w["norm_mix"], w["w_in"], tabs, w["gains"], TM_PROJ,
                  DA_QK_DIM ** -0.5 * LOG2E, DSA_HEAD_DIM ** -0.5 * LOG2E)
    ka, va, kd, vd = _cols(z, "ka"), _cols(z, "va"), _cols(z, "kd"), _cols(z, "vd")
    kw = _cols(z, "kw")
    ki = kw[:, :IDX_DIM]
    oa = _diff_attn(_cols(zb, "qa"), _key_tiles(_cols(zb, "ka"), DA_KV_HEADS),
                    _with_ones(_cols(zb, "va"), DA_KV_HEADS), w["lp"], lam_init, t_real)
    kib = _cols(zb, "kw")[:, :IDX_DIM]
    ki2 = _key_tiles(jnp.concatenate([kib, kib], axis=1), 1)[0]
    wi = kw[:, IDX_DIM:IDX_DIM + LANES]
    bias = _select(_cols(zb, "qi"), wi, ki2, topk, t_real)
    od = _dsa_attn(_cols(zb, "qd"), _key_tiles(_cols(zb, "kd"), DSA_KV_HEADS),
                   _with_ones(_cols(zb, "vd"), DSA_KV_HEADS), bias, t_real)
    x = _merge(x, oa, od, z, w["out_norm"], w["wa"], w["wb"], w["wo"], lam_init, TM_MERGE)
    x = _ffn(x, w["norm_ffn2"], w["w_gu2"], w["w_d2"], TM_FFN)
    return x, (ka, va, kd, vd, ki)


def _expand_queries(q, n_kv, per_kv_rows, width, scale):
    nb = q.shape[0]
    eye = jnp.eye(n_kv, dtype=q.dtype)
    out = jnp.einsum("bkrw,kc->bkrcw", q * scale, eye).reshape(nb, n_kv * per_kv_rows, n_kv * width)
    pad = STREAM_ROWS - n_kv * per_kv_rows
    return jnp.pad(out, ((0, 0), (0, pad), (0, 0))).astype(BF16)


def _cache_views(cache_da_k, cache_da_v, cache_dsa_k, cache_dsa_v, cache_idx_k):
    depth, n_pool = cache_da_k.shape[:2]
    rows = lambda c: c.reshape(depth, n_pool, -1, c.shape[-1])
    return (jnp.transpose(cache_da_k, (0, 1, 3, 4, 5, 2)), rows(cache_da_v), rows(cache_dsa_k),
            rows(cache_dsa_v), jnp.transpose(cache_idx_k, (0, 1, 3, 2)))


def _sample_layer(x, w, tabs, lam_init, topk, layer, caches, page_table):
    da_kt, da_v, dsa_k, dsa_v, idx_kt = caches
    nb = page_table.shape[0]
    rows = x.shape[0]
    x = _ffn(x, w["norm_ffn1"], w["w_gu1"], w["w_d1"], rows)
    z, _ = _proj(x, w["norm_mix"], w["w_in"], tabs, w["gains"], rows, None, None)
    ka, va, kd, vd = (_cols(z, n)[:nb] for n in ("ka", "va", "kd", "vd"))
    kw = _cols(z, "kw")[:nb]
    ki = kw[:, :IDX_DIM]
    wi = kw[:, IDX_DIM:IDX_DIM + IDX_HEADS]
    qa = _cols(z, "qa")[:nb].reshape(nb, DA_KV_HEADS, 2, 2, DA_QK_DIM)
    blk = jnp.einsum("bkgmd,mn->bkgmnd", qa * DA_QK_DIM ** -0.5, jnp.eye(2, dtype=F32))
    wq_da = _expand_queries(blk.reshape(nb, DA_KV_HEADS, 4, 2 * DA_QK_DIM), DA_KV_HEADS, 4, LANES, 1.0)
    oa = _paged_attn("da", layer, page_table, wq_da, da_kt, da_v,
                     ka.reshape(nb, 1, -1), va.reshape(nb, 1, -1), w["lp"], lam_init)
    qi = _cols(z, "qi")[:nb]
    scores = _paged_index(layer, page_table, qi.reshape(nb, IDX_HEADS, IDX_DIM).astype(BF16),
                          wi.reshape(nb, IDX_HEADS, 1), idx_kt)
    bias = _sample_select(scores.reshape(nb, -1), qi, jnp.tile(ki, (1, IDX_HEADS)),
                          jnp.pad(wi, ((0, 0), (0, LANES - IDX_HEADS))), topk)
    qd = _cols(z, "qd")[:nb].reshape(nb, DSA_KV_HEADS, 2, DSA_HEAD_DIM)
    wq_dsa = _expand_queries(qd, DSA_KV_HEADS, 2, DSA_HEAD_DIM, DSA_HEAD_DIM ** -0.5)
    od = _paged_attn("dsa", layer, page_table, wq_dsa, dsa_k, dsa_v,
                     kd.reshape(nb, 1, -1), vd.reshape(nb, 1, -1), bias.reshape(nb, 1, -1), None)
    pad = ((0, rows - nb), (0, 0))
    x = _merge(x, jnp.pad(oa.reshape(nb, -1), pad), jnp.pad(od.reshape(nb, -1), pad), z,
               w["out_norm"], w["wa"], w["wb"], w["wo"], lam_init, rows)
    x = _ffn(x, w["norm_ffn2"], w["w_gu2"], w["w_d2"], rows)
    return x, (ka, va, kd, vd, ki)


def kernel(x_prompt, x_sample, cache_da_k, cache_da_v, cache_dsa_k, cache_dsa_v, cache_idx_k, page_table,
           meta_tokens, norm_ffn1, ffn1_w_gu, ffn1_w_down, norm_mix, w_in, da_q_norm, da_k_norm,
           dsa_q_norm, dsa_k_norm, da_lambda, da_out_norm, w_branch_a, w_branch_b, w_out,
           norm_ffn2, ffn2_w_gu, ffn2_w_down):
    params = dict(norm_ffn1=norm_ffn1, ffn1_w_gu=ffn1_w_gu, ffn1_w_down=ffn1_w_down, norm_mix=norm_mix,
                  w_in=w_in, da_q_norm=da_q_norm, da_k_norm=da_k_norm, dsa_q_norm=dsa_q_norm,
                  dsa_k_norm=dsa_k_norm, da_lambda=da_lambda, da_out_norm=da_out_norm,
                  w_branch_a=w_branch_a, w_branch_b=w_branch_b, w_out=w_out, norm_ffn2=norm_ffn2,
                  ffn2_w_gu=ffn2_w_gu, ffn2_w_down=ffn2_w_down)
    depth, d_model = norm_mix.shape
    assert w_in.shape[2] == sum(SPLITS) + 2 * d_model and d_model == DA_HEADS * DA_V_DIM
    assert x_prompt.shape[0] == 1 and x_sample.shape[1] == 1
    weights = [_layer_weights(l, params) for l in range(depth)]
    lam_inits = [0.8 - 0.6 * math.exp(-0.3 * l) for l in range(depth)]

    seq = x_prompt.shape[1]
    t = seq + N_META
    t_pad = -(-t // T_ALIGN) * T_ALIGN
    xp = jnp.concatenate([meta_tokens.astype(F32), x_prompt[0], jnp.zeros((t_pad - t, d_model), F32)], axis=0)
    pos_p = jnp.arange(t_pad, dtype=jnp.int32)
    tabs_p = _rope_tables(pos_p, DA_QK_DIM) + _rope_tables(pos_p, DSA_HEAD_DIM)
    topk_p = min(TOPK_MAX, t // 4)
    rows_p = []
    for l in range(depth):
        xp, rows = _prompt_layer(xp, weights[l], tabs_p, lam_inits[l], topk_p, t)
        rows_p.append(tuple(r[:t] for r in rows))
    y_prompt = xp[N_META:t][None]

    nb, n_pages = page_table.shape
    past = n_pages * PAGE_SIZE
    rows_s_pad = -(-nb // 16) * 16
    xs = jnp.pad(x_sample[:, 0, :], ((0, rows_s_pad - nb), (0, 0)))
    pos_s = jnp.full((rows_s_pad,), past, jnp.int32)
    tabs_s = _rope_tables(pos_s, DA_QK_DIM) + _rope_tables(pos_s, DSA_HEAD_DIM)
    topk_s = min(TOPK_MAX, (past + 1) // 4)
    rows_s = []
    caches = _cache_views(cache_da_k, cache_da_v, cache_dsa_k, cache_dsa_v, cache_idx_k)
    for l in range(depth):
        xs, rows = _sample_layer(xs, weights[l], tabs_s, lam_inits[l], topk_s, l, caches, page_table)
        rows_s.append(rows)
    y_sample = xs[:nb][:, None, :]

    def stack(rows, i, shape):
        return jnp.stack([r[i] for r in rows], axis=0).reshape((depth,) + shape)

    lead_p, lead_s = (1, t), (nb, 1)
    return (y_prompt, y_sample,
            stack(rows_p, 0, lead_p + (DA_KV_HEADS, 2, DA_QK_DIM)), stack(rows_p, 1, lead_p + (DA_KV_HEADS, DA_V_DIM)),
            stack(rows_p, 2, lead_p + (DSA_KV_HEADS, DSA_HEAD_DIM)), stack(rows_p, 3, lead_p + (DSA_KV_HEADS, DSA_HEAD_DIM)),
            stack(rows_p, 4, lead_p + (IDX_DIM,)),
            stack(rows_s, 0, lead_s + (DA_KV_HEADS, 2, DA_QK_DIM)), stack(rows_s, 1, lead_s + (DA_KV_HEADS, DA_V_DIM)),
            stack(rows_s, 2, lead_s + (DSA_KV_HEADS, DSA_HEAD_DIM)), stack(rows_s, 3, lead_s + (DSA_KV_HEADS, DSA_HEAD_DIM)),
            stack(rows_s, 4, lead_s + (IDX_DIM,)))
```

```python
import functools
import math

import jax
import jax.numpy as jnp
import numpy as np
from jax import lax
from jax.experimental import pallas as pl
from jax.experimental.pallas import tpu as pltpu

F32 = jnp.float32
BF16 = jnp.bfloat16
I32 = jnp.int32

N_META = 16
ROPE_THETA = 500000.0
EPS = 1e-6
PAGE_SIZE = 128
DA_HEADS, DA_KV_HEADS, DA_QK_DIM, DA_V_DIM = 8, 4, 64, 128
DSA_HEADS, DSA_KV_HEADS, DSA_HEAD_DIM = 8, 4, 128
IDX_HEADS, IDX_DIM = 8, 64
TOPK_MAX = 256
SPLITS = (DA_HEADS * 2 * DA_QK_DIM, DA_KV_HEADS * 2 * DA_QK_DIM, DA_KV_HEADS * DA_V_DIM,
          DSA_HEADS * DSA_HEAD_DIM, DSA_KV_HEADS * DSA_HEAD_DIM, DSA_KV_HEADS * DSA_HEAD_DIM,
          IDX_HEADS * IDX_DIM, IDX_DIM, IDX_HEADS)

LANES = 128
MXU_COLS = 256
VMEM_LIMIT_BYTES = 56 * 1024 * 1024

PROJ_TN = 2 * MXU_COLS
_OFF = {}
_o = 0
for _name, _w in (("qa", 1024), ("ka", 512), ("va", 512), ("qd", 1024), ("kd", 512), ("vd", 512),
                  ("ga", 1024), ("gb", 1024), ("qi", 512), ("kw", PROJ_TN)):
    _OFF[_name] = (_o, _w)
    _o += _w
PROJ_N = _o

LOG2E = math.log2(math.e)
NEG = -1e30
INT_MIN = -2 ** 31

TQ = 512
TK = 512
TQ_SEL = 256
SEL_CHUNK = 32
SHORTLIST = 12
T_ALIGN = 1536
TM_FFN = 768
TM_PROJ = 768
TM_MERGE = 512
FF_CHUNK = 256
PAGES_PER_STEP = 16
INDEX_PAGES_PER_STEP = 32


def _cparams(*sem):
    return pltpu.CompilerParams(dimension_semantics=sem, vmem_limit_bytes=VMEM_LIMIT_BYTES)


def _ffn_kernel(x_ref, g_ref, wg_ref, wu_ref, wd_ref, o_ref, h_scr):
    c = pl.program_id(1)

    @pl.when(c == 0)
    def _():
        x = x_ref[...]
        ms = jnp.mean(x * x, axis=-1, keepdims=True)
        h_scr[...] = ((x * lax.rsqrt(ms + EPS)) * g_ref[...]).astype(BF16)
        o_ref[...] = x

    h = h_scr[...]
    g = jnp.dot(h, wg_ref[...], preferred_element_type=F32)
    u = jnp.dot(h, wu_ref[...], preferred_element_type=F32)
    a = ((g * jax.nn.sigmoid(g)) * u).astype(BF16)
    o_ref[...] += 0.5 * jnp.dot(a, wd_ref[...], preferred_element_type=F32)


def _ffn(x, gain, w_gu, w_down, tm):
    m, d = x.shape
    d_ff = w_down.shape[0]
    n_c = d_ff // FF_CHUNK
    return pl.pallas_call(
        _ffn_kernel,
        out_shape=jax.ShapeDtypeStruct((m, d), F32),
        grid=(m // tm, n_c),
        in_specs=[
            pl.BlockSpec((tm, d), lambda i, c: (i, 0)),
            pl.BlockSpec((1, d), lambda i, c: (0, 0)),
            pl.BlockSpec((d, FF_CHUNK), lambda i, c: (0, c)),
            pl.BlockSpec((d, FF_CHUNK), lambda i, c: (0, c + n_c)),
            pl.BlockSpec((FF_CHUNK, d), lambda i, c: (c, 0)),
        ],
        out_specs=pl.BlockSpec((tm, d), lambda i, c: (i, 0)),
        scratch_shapes=[pltpu.VMEM((tm, d), BF16)],
        compiler_params=_cparams("parallel", "arbitrary"),
        name="ffn",
    )(x, gain.reshape(1, d), w_gu, w_gu, w_down)


def _seg_rms(x, gain, seg):
    cols = []
    for b in range(x.shape[1] // LANES):
        blk = x[:, b * LANES:(b + 1) * LANES]
        sq = blk * blk
        if seg == LANES:
            ms = jnp.sum(sq, axis=-1, keepdims=True) * (1.0 / seg)
        else:
            lane = lax.broadcasted_iota(I32, blk.shape, 1)
            lo = jnp.sum(jnp.where(lane < seg, sq, 0.0), axis=-1, keepdims=True)
            hi = jnp.sum(jnp.where(lane >= seg, sq, 0.0), axis=-1, keepdims=True)
            ms = jnp.where(lane < seg, lo, hi) * (1.0 / seg)
        cols.append((blk * lax.rsqrt(ms + EPS)) * gain)
    return jnp.concatenate(cols, axis=1)


def _rope(x, cos, sin, half):
    n = x.shape[1]
    reps = n // LANES
    c = jnp.concatenate([cos] * reps, axis=1)
    s = jnp.concatenate([sin] * reps, axis=1)
    up = pltpu.roll(x, n - half, 1)
    dn = pltpu.roll(x, half, 1)
    lane = lax.broadcasted_iota(I32, x.shape, 1)
    partner = jnp.where((lane & (2 * half - 1)) < half, up, dn)
    return x * c + partner * s


def _proj_kernel(qa_scale, qd_scale, x_ref, g_ref, w_ref, c64_ref, s64_ref, c128_ref, s128_ref,
                 qa_g, ka_g, qd_g, kd_g, o_ref, ob_ref, h_scr):
    j = pl.program_id(1)

    def emit(r, scale=None):
        o_ref[...] = r
        ob_ref[...] = (r if scale is None else r * scale).astype(BF16)

    @pl.when(j == 0)
    def _():
        x = x_ref[...]
        ms = jnp.mean(x * x, axis=-1, keepdims=True)
        h_scr[...] = ((x * lax.rsqrt(ms + EPS)) * g_ref[...]).astype(BF16)

    z = jnp.dot(h_scr[...], w_ref[...], preferred_element_type=F32)

    def tile_of(name):
        off, width = _OFF[name]
        return off // PROJ_TN, (off + width) // PROJ_TN

    def in_group(name):
        lo, hi = tile_of(name)
        return jnp.logical_and(j >= lo, j < hi)

    h64, h128 = DA_QK_DIM // 8, DSA_HEAD_DIM // 8

    @pl.when(in_group("qa"))
    def _():
        emit(_rope(_seg_rms(z, qa_g[...], DA_QK_DIM), c64_ref[...], s64_ref[...], h64), qa_scale)

    @pl.when(in_group("ka"))
    def _():
        emit(_rope(_seg_rms(z, ka_g[...], DA_QK_DIM), c64_ref[...], s64_ref[...], h64))

    @pl.when(in_group("qd"))
    def _():
        emit(_rope(_seg_rms(z, qd_g[...], DSA_HEAD_DIM), c128_ref[...], s128_ref[...], h128), qd_scale)

    @pl.when(in_group("kd"))
    def _():
        emit(_rope(_seg_rms(z, kd_g[...], DSA_HEAD_DIM), c128_ref[...], s128_ref[...], h128))

    @pl.when(jnp.logical_or(in_group("va"), in_group("vd")))
    def _():
        emit(z)

    @pl.when(in_group("qi"))
    def _():
        emit(_rope(z, c64_ref[...], s64_ref[...], h64))

    @pl.when(jnp.logical_or(in_group("ga"), in_group("gb")))
    def _():
        emit(jax.nn.sigmoid(z))

    @pl.when(in_group("kw"))
    def _():
        r = _rope(z, c64_ref[...], s64_ref[...], h64)
        lane = lax.broadcasted_iota(I32, z.shape, 1)
        wi_scale = IDX_HEADS ** -0.5 * IDX_DIM ** -0.5
        emit(jnp.where(lane < IDX_DIM, r, z * wi_scale))


def _proj(x, gain, w_packed, tabs, gains, tm, qa_scale, qd_scale):
    m, d = x.shape
    n = w_packed.shape[1]
    c64, s64, c128, s128 = tabs
    row = lambda i, j: (i, 0)
    const = lambda i, j: (0, 0)
    return pl.pallas_call(
        functools.partial(_proj_kernel, qa_scale, qd_scale),
        out_shape=(jax.ShapeDtypeStruct((m, n), F32), jax.ShapeDtypeStruct((m, n), BF16)),
        grid=(m // tm, n // PROJ_TN),
        in_specs=[
            pl.BlockSpec((tm, d), row),
            pl.BlockSpec((1, d), const),
            pl.BlockSpec((d, PROJ_TN), lambda i, j: (0, j)),
            pl.BlockSpec((tm, LANES), row), pl.BlockSpec((tm, LANES), row),
            pl.BlockSpec((tm, LANES), row), pl.BlockSpec((tm, LANES), row),
            pl.BlockSpec((1, LANES), const), pl.BlockSpec((1, LANES), const),
            pl.BlockSpec((1, LANES), const), pl.BlockSpec((1, LANES), const),
        ],
        out_specs=(pl.BlockSpec((tm, PROJ_TN), lambda i, j: (i, j)),
                   pl.BlockSpec((tm, PROJ_TN), lambda i, j: (i, j))),
        scratch_shapes=[pltpu.VMEM((tm, d), BF16)],
        compiler_params=_cparams("parallel", "arbitrary"),
        name="proj",
    )(x, gain.reshape(1, d), w_packed, c64, s64, c128, s128, *gains)


def _merge_kernel(lam_init, x_ref, oa_ref, od_ref, ga_ref, gb_ref, g_ref, wa_ref, wb_ref, wo_ref,
                  o_ref, a_scr):
    for h in range(DA_HEADS):
        blk = oa_ref[:, h * DA_V_DIM:(h + 1) * DA_V_DIM]
        ms = jnp.mean(blk * blk, axis=-1, keepdims=True)
        y = ((blk * lax.rsqrt(ms + EPS)) * g_ref[...]) * (1.0 - lam_init)
        a_scr[:, h * DA_V_DIM:(h + 1) * DA_V_DIM] = y.astype(BF16)
    a = jnp.dot(a_scr[...], wa_ref[...], preferred_element_type=F32)
    b = jnp.dot(od_ref[...].astype(BF16), wb_ref[...], preferred_element_type=F32)
    merged = ga_ref[...] * a + gb_ref[...] * b
    o_ref[...] = x_ref[...] + jnp.dot(merged.astype(BF16), wo_ref[...], preferred_element_type=F32)


def _merge(x, oa, od, z, gain, wa, wb, wo, lam_init, tm):
    m, d = x.shape
    row = lambda i: (i, 0)
    const = lambda i: (0, 0)
    ga_blk = _OFF["ga"][0] // d
    gb_blk = _OFF["gb"][0] // d
    return pl.pallas_call(
        functools.partial(_merge_kernel, lam_init),
        out_shape=jax.ShapeDtypeStruct((m, d), F32),
        grid=(m // tm,),
        in_specs=[
            pl.BlockSpec((tm, d), row), pl.BlockSpec((tm, d), row), pl.BlockSpec((tm, d), row),
            pl.BlockSpec((tm, d), lambda i: (i, ga_blk)), pl.BlockSpec((tm, d), lambda i: (i, gb_blk)),
            pl.BlockSpec((1, DA_V_DIM), const),
            pl.BlockSpec((d, d), const), pl.BlockSpec((d, d), const), pl.BlockSpec((d, d), const),
        ],
        out_specs=pl.BlockSpec((tm, d), row),
        scratch_shapes=[pltpu.VMEM((tm, d), BF16)],
        compiler_params=_cparams("parallel"),
        name="merge",
    )(x, oa, od, z, z, gain.reshape(1, DA_V_DIM), wa, wb, wo)


def _lambda_of(lp, lam_init):
    a = jnp.sum(lp[0:1, :] * lp[1:2, :], axis=-1, keepdims=True)
    b = jnp.sum(lp[2:3, :] * lp[3:4, :], axis=-1, keepdims=True)
    return jnp.exp(a) - jnp.exp(b) + lam_init


def _softmax_step(s, v, m_scr, l_scr, acc_scr):
    m_prev = m_scr[...]
    m_new = jnp.maximum(m_prev, jnp.max(s, axis=-1, keepdims=True))
    alpha = jnp.exp(m_prev - m_new)
    p = jnp.exp(s - m_new)
    l_scr[...] = alpha * l_scr[...] + jnp.sum(p, axis=-1, keepdims=True)
    acc_scr[...] = alpha * acc_scr[...] + jnp.dot(p.astype(BF16), v, preferred_element_type=F32)
    m_scr[...] = m_new


def _flash_step(s, v_ext, m_ref, acc_ref):
    m_prev = m_ref[...]
    m_new = jnp.maximum(m_prev, jnp.max(s, axis=-1, keepdims=True))
    alpha = jnp.exp2(m_prev - m_new)
    p = jnp.exp2(s - jnp.concatenate([m_new] * (s.shape[1] // LANES), axis=1))
    acc_ref[...] = (jnp.concatenate([alpha] * (acc_ref.shape[1] // LANES), axis=1) * acc_ref[...]
                    + jnp.dot(p.astype(BF16), v_ext, preferred_element_type=F32))
    m_ref[...] = m_new


def _diff_attn_kernel(lam_init, tail, q_ref, kt_ref, v_ref, lp_ref, o_ref, lhs_scr, m_scr, acc_scr):
    qt = pl.program_id(1)

    def run(tq):
        rows = 4 * tq
        q = q_ref[0:tq, :]
        lane = lax.broadcasted_iota(I32, (tq, LANES), 1)
        for g in range(2):
            qg = q[:, g * LANES:(g + 1) * LANES]
            for mp in range(2):
                keep = (lane < DA_QK_DIM) if mp == 0 else (lane >= DA_QK_DIM)
                r = (g * 2 + mp) * tq
                lhs_scr[r:r + tq, :] = jnp.where(keep, qg, jnp.zeros_like(qg))
        m_ref, acc_ref = m_scr.at[0:rows], acc_scr.at[0:rows]
        m_ref[...] = jnp.full((rows, LANES), NEG, F32)
        acc_ref[...] = jnp.zeros((rows, 2 * DA_V_DIM), F32)

        def step(kt, masked):
            s = jnp.dot(lhs_scr[0:rows, :], kt_ref[0, kt], preferred_element_type=F32)
            if masked:
                qpos = qt * TQ + (lax.broadcasted_iota(I32, (rows, TK), 0) & (tq - 1))
                col = lax.broadcasted_iota(I32, (rows, TK), 1) + kt * TK
                s = jnp.where(col <= qpos, s, NEG)
            v = v_ref[pl.ds(pl.multiple_of(kt * TK, TK), TK), :]
            _flash_step(s, v, m_ref, acc_ref)

        def body(kt, carry):
            step(kt, False)
            return carry

        lax.fori_loop(0, qt, body, 0)
        step(qt, True)
        acc = acc_ref[...]
        o = acc[:, :DA_V_DIM] / acc[:, DA_V_DIM:]
        lam = _lambda_of(lp_ref[...], lam_init)
        for g in range(2):
            r = g * 2 * tq
            o_ref[0:tq, g * DA_V_DIM:(g + 1) * DA_V_DIM] = o[r:r + tq, :] - lam * o[r + tq:r + 2 * tq, :]
        if tq < TQ:
            o_ref[tq:TQ, :] = jnp.zeros((TQ - tq, o_ref.shape[1]), F32)

    if tail is None:
        run(TQ)
    else:
        pl.when(qt != tail[0])(lambda: run(TQ))
        pl.when(qt == tail[0])(lambda: run(tail[1]))


def _tail_tile(t_real, t_pad):
    rem = t_real % TQ
    if t_pad - t_real >= TQ or rem == 0 or rem > TQ // 4:
        return None
    return t_real // TQ, max(16, pl.next_power_of_2(rem))


def _diff_attn(q, kt, v_ext, lp, lam_init, t_real):
    t = q.shape[0]
    n_kt = t // TK
    assert TQ == TK
    return pl.pallas_call(
        functools.partial(_diff_attn_kernel, lam_init, _tail_tile(t_real, t)),
        out_shape=jax.ShapeDtypeStruct((t, DA_HEADS * DA_V_DIM), F32),
        grid=(DA_KV_HEADS, t // TQ),
        in_specs=[
            pl.BlockSpec((TQ, 2 * LANES), lambda kv, i: (i, kv)),
            pl.BlockSpec((1, n_kt, LANES, TK), lambda kv, i: (kv, 0, 0, 0)),
            pl.BlockSpec((t, 2 * DA_V_DIM), lambda kv, i: (0, kv)),
            pl.BlockSpec((4, DA_QK_DIM), lambda kv, i: (0, 0)),
        ],
        out_specs=pl.BlockSpec((TQ, 2 * DA_V_DIM), lambda kv, i: (i, kv)),
        scratch_shapes=[pltpu.VMEM((4 * TQ, LANES), BF16), pltpu.VMEM((4 * TQ, LANES), F32),
                        pltpu.VMEM((4 * TQ, 2 * DA_V_DIM), F32)],
        compiler_params=_cparams("parallel", "arbitrary"),
        name="diff_attn",
    )(q, kt, v_ext, lp)


def _float_key(s):
    s = jnp.where(s == 0.0, 0.0, s)
    bits = pltpu.bitcast(s, I32)
    return jnp.where(bits < 0, bits ^ jnp.int32(0x7FFFFFFF), bits)


def _key_float(key):
    bits = jnp.where(key < 0, key ^ jnp.int32(0x7FFFFFFF), key)
    return pltpu.bitcast(bits, F32)


KEY_LOWEST_FINITE = INT_MIN + 0x00800000


def _kth_largest_key(count_ge, rows, k):
    def bit_body(i, t):
        cand = t + lax.shift_left(jnp.int32(1), 31 - i)
        return jnp.where(count_ge(cand) >= k, cand, t)

    return lax.fori_loop(0, 32, bit_body, jnp.full((rows, 1), INT_MIN, I32))


def _select_kernel(topk, t_real, qi_ref, wi_ref, ki_ref, o_ref, lhs_scr, sc_scr, short_scr, t_scr, n_scr):
    qt = pl.program_id(0)
    n_kt = sc_scr.shape[0]
    n_rows = jnp.where(qt * TQ_SEL < t_real, jnp.minimum((qt + 1) * TQ_SEL, t_real), 1)
    n_k = (n_rows + TK - 1) // TK
    lane = lax.broadcasted_iota(I32, (TQ_SEL, LANES), 1)
    n_chunk = TQ_SEL // SEL_CHUNK
    for h in range(IDX_HEADS):
        pair = qi_ref[:, (h // 2) * LANES:(h // 2 + 1) * LANES]
        keep = (lane < IDX_DIM) if h % 2 == 0 else (lane >= IDX_DIM)
        masked = jnp.where(keep, pair, jnp.zeros_like(pair))
        for c in range(n_chunk):
            r = (c * IDX_HEADS + h) * SEL_CHUNK
            lhs_scr[r:r + SEL_CHUNK, :] = masked[c * SEL_CHUNK:(c + 1) * SEL_CHUNK, :]
    wi = wi_ref[...]
    w_cols = [wi[:, h:h + 1] for h in range(IDX_HEADS)]
    qpos = qt * TQ_SEL + lax.broadcasted_iota(I32, (TQ_SEL, TK), 0)
    qpos = jnp.where(qpos < t_real, qpos, 0)
    col = lax.broadcasted_iota(I32, (TQ_SEL, TK), 1)

    def score_body(kt, carry):
        d = jnp.dot(lhs_scr[...], ki_ref[kt], preferred_element_type=F32)
        for c in range(n_chunk):
            rows = slice(c * SEL_CHUNK, (c + 1) * SEL_CHUNK)
            w = wi_ref[rows, :]
            sc = jnp.zeros((SEL_CHUNK, TK), F32)
            for h in range(IDX_HEADS):
                r = (c * IDX_HEADS + h) * SEL_CHUNK
                sc = sc + jnp.maximum(d[r:r + SEL_CHUNK, :], 0.0) * w[:, h:h + 1]
            sc = jnp.where(sc == 0.0, 0.0, sc)
            qp = qt * TQ_SEL + c * SEL_CHUNK + lax.broadcasted_iota(I32, (SEL_CHUNK, TK), 0)
            qp = jnp.where(qp < t_real, qp, 0)
            kp = lax.broadcasted_iota(I32, (SEL_CHUNK, TK), 1) + kt * TK
            sc_scr[kt, rows, :] = jnp.where(kp <= qp, sc, -jnp.inf)
        return carry

    lax.fori_loop(0, n_k, score_body, 0)

    def count(pred):
        def body(kt, acc):
            hit = jnp.where(pred(sc_scr[kt], col + kt * TK), 1.0, 0.0)
            for c in range(TK // LANES):
                acc = acc + hit[:, c * LANES:(c + 1) * LANES]
            return acc

        acc = lax.fori_loop(0, n_k, body, jnp.zeros((TQ_SEL, LANES), F32))
        return jnp.sum(acc, axis=-1, keepdims=True)

    def shortlist_body(rg, carry):
        rows8 = [pl.ds(pl.multiple_of(rg * 16 + 8 * u, 8), 8) for u in range(2)]

        def insert_body(kt, tops):
            tops = list(tops)
            for c in range(TK // LANES):
                for u in range(2):
                    x = sc_scr[kt, rows8[u], c * LANES:(c + 1) * LANES]
                    for j in range(SHORTLIST):
                        top = tops[u * SHORTLIST + j]
                        tops[u * SHORTLIST + j] = jnp.maximum(top, x)
                        x = jnp.minimum(top, x)
            return tuple(tops)

        tops = lax.fori_loop(0, n_k, insert_body, (jnp.full((8, LANES), -jnp.inf, F32),) * (2 * SHORTLIST))
        for u in range(2):
            for j in range(SHORTLIST):
                short_scr[j, rows8[u], :] = _float_key(tops[u * SHORTLIST + j])
        return carry

    lax.fori_loop(0, TQ_SEL // 16, shortlist_body, 0)

    def count_short(cand):
        acc = jnp.zeros((TQ_SEL, LANES), F32)
        for j in range(SHORTLIST):
            acc = acc + jnp.where(short_scr[j] >= cand, 1.0, 0.0)
        return jnp.sum(acc, axis=-1, keepdims=True)

    kf = float(topk)
    t_key = jnp.maximum(_kth_largest_key(count_short, TQ_SEL, kf), KEY_LOWEST_FINITE)
    n_ge = count(lambda s, idx: s >= _key_float(t_key))
    t_scr[...] = t_key
    n_scr[...] = n_ge

    @pl.when(jnp.max(jnp.abs(n_ge - count_short(t_key))) > 0.0)
    def _():
        t_full = _kth_largest_key(lambda cand: count(lambda s, idx: s >= _key_float(cand)), TQ_SEL, kf)
        t_full = jnp.maximum(t_full, KEY_LOWEST_FINITE)
        t_scr[...] = t_full
        n_scr[...] = count(lambda s, idx: s >= _key_float(t_full))

    t = _key_float(t_scr[...])
    n_ge = n_scr[...]

    @pl.when(jnp.max(n_ge) > kf)
    def _():
        need = kf - count(lambda s, idx: s > t)

        def bit_body(i, x):
            cand = x + lax.shift_left(jnp.int32(1), 14 - i)
            below = count(lambda s, idx: jnp.logical_and(s == t, idx < cand))
            return jnp.where(below < need, cand, x)

        last = lax.fori_loop(0, 15, bit_body, jnp.zeros((TQ_SEL, 1), I32))

        def drop_body(kt, carry):
            s = sc_scr[kt]
            drop = jnp.logical_and(s == t, col + kt * TK > last)
            sc_scr[kt] = jnp.where(drop, -jnp.inf, s)
            return carry

        lax.fori_loop(0, n_k, drop_body, 0)

    def write_body(kt, carry):
        o_ref[kt] = jnp.where(sc_scr[kt] >= t, 0.0, NEG).astype(BF16)
        return carry

    lax.fori_loop(0, n_k, write_body, 0)

    def fill_body(kt, carry):
        o_ref[kt] = jnp.full((TQ_SEL, TK), NEG, BF16)
        return carry

    lax.fori_loop(n_k, n_kt, fill_body, 0)


def _select(qi, wi, ki2, topk, t_real):
    t = qi.shape[0]
    n_kt = t // TK
    assert t < 2 ** 15
    return pl.pallas_call(
        functools.partial(_select_kernel, topk, t_real),
        out_shape=jax.ShapeDtypeStruct((n_kt, t, TK), BF16),
        grid=(t // TQ_SEL,),
        in_specs=[
            pl.BlockSpec((TQ_SEL, IDX_HEADS * IDX_DIM), lambda i: (i, 0)),
            pl.BlockSpec((TQ_SEL, LANES), lambda i: (i, 0)),
            pl.BlockSpec((n_kt, LANES, TK), lambda i: (0, 0, 0)),
        ],
        out_specs=pl.BlockSpec((n_kt, TQ_SEL, TK), lambda i: (0, i, 0)),
        scratch_shapes=[pltpu.VMEM((IDX_HEADS * TQ_SEL, LANES), BF16),
                        pltpu.VMEM((n_kt, TQ_SEL, TK), F32),
                        pltpu.VMEM((SHORTLIST, TQ_SEL, LANES), I32),
                        pltpu.VMEM((TQ_SEL, 1), I32), pltpu.VMEM((TQ_SEL, 1), F32)],
        compiler_params=_cparams("parallel"),
        name="dsa_select",
    )(qi, wi, ki2)


def _dsa_attn_kernel(tail, qt_ref, kt_ref_s, q_ref, kt_ref, v_ref, b_ref, o_ref, lhs_scr, m_scr, acc_scr):
    step = pl.program_id(0)
    qt, kt = qt_ref[step], kt_ref_s[step]

    def run(tq):
        rows = 2 * tq

        @pl.when(kt == 0)
        def _():
            for kv in range(DSA_KV_HEADS):
                for g in range(2):
                    h = kv * 2 + g
                    lhs_scr[kv, g * tq:(g + 1) * tq, :] = q_ref[0:tq, h * LANES:(h + 1) * LANES]
            m_scr[:, 0:rows, :] = jnp.full((DSA_KV_HEADS, rows, LANES), NEG, F32)
            acc_scr[:, 0:rows, :] = jnp.zeros((DSA_KV_HEADS, rows, 2 * DSA_HEAD_DIM), F32)

        bias = b_ref[0, 0:tq, :].astype(F32)
        bias2 = jnp.concatenate([bias, bias], axis=0)
        for kv in range(DSA_KV_HEADS):
            s = jnp.dot(lhs_scr[kv, 0:rows, :], kt_ref[kv, 0], preferred_element_type=F32) + bias2
            v = v_ref[:, kv * 2 * DSA_HEAD_DIM:(kv + 1) * 2 * DSA_HEAD_DIM]
            _flash_step(s, v, m_scr.at[kv, 0:rows], acc_scr.at[kv, 0:rows])

        @pl.when(kt == qt)
        def _():
            for kv in range(DSA_KV_HEADS):
                acc = acc_scr[kv, 0:rows, :]
                o = acc[:, :DSA_HEAD_DIM] / acc[:, DSA_HEAD_DIM:]
                for g in range(2):
                    h = kv * 2 + g
                    o_ref[0:tq, h * DSA_HEAD_DIM:(h + 1) * DSA_HEAD_DIM] = o[g * tq:(g + 1) * tq, :]
            if tq < TQ:
                o_ref[tq:TQ, :] = jnp.zeros((TQ - tq, o_ref.shape[1]), F32)

    if tail is None:
        run(TQ)
    else:
        pl.when(qt != tail[0])(lambda: run(TQ))
        pl.when(qt == tail[0])(lambda: run(tail[1]))


def _dsa_attn(q, kt, v_ext, bias, t_real):
    t = q.shape[0]
    assert TQ == TK
    kv, d = DSA_KV_HEADS, DSA_HEAD_DIM
    n_t = t // TQ
    qt_of_step = np.concatenate([np.full(i + 1, i, np.int32) for i in range(n_t)])
    kt_of_step = np.concatenate([np.arange(i + 1, dtype=np.int32) for i in range(n_t)])
    grid_spec = pltpu.PrefetchScalarGridSpec(
        num_scalar_prefetch=2,
        grid=(qt_of_step.shape[0],),
        in_specs=[
            pl.BlockSpec((TQ, DSA_HEADS * d), lambda s, qt, kt: (qt[s], 0)),
            pl.BlockSpec((kv, 1, LANES, TK), lambda s, qt, kt: (0, kt[s], 0, 0)),
            pl.BlockSpec((TK, kv * 2 * d), lambda s, qt, kt: (kt[s], 0)),
            pl.BlockSpec((1, TQ, TK), lambda s, qt, kt: (kt[s], qt[s], 0)),
        ],
        out_specs=pl.BlockSpec((TQ, DSA_HEADS * d), lambda s, qt, kt: (qt[s], 0)),
        scratch_shapes=[pltpu.VMEM((kv, 2 * TQ, LANES), BF16), pltpu.VMEM((kv, 2 * TQ, LANES), F32),
                        pltpu.VMEM((kv, 2 * TQ, 2 * d), F32)],
    )
    return pl.pallas_call(
        functools.partial(_dsa_attn_kernel, _tail_tile(t_real, t)),
        out_shape=jax.ShapeDtypeStruct((t, DSA_HEADS * d), F32),
        grid_spec=grid_spec,
        compiler_params=_cparams("arbitrary"),
        name="dsa_attn",
    )(jnp.asarray(qt_of_step), jnp.asarray(kt_of_step), q, kt, v_ext, bias)


STREAM_ROWS = 16


def _paged_attn_kernel(mode, lam_init, n_pages, npg, *refs):
    pt_ref, wq_ref = refs[0], refs[1]
    k_refs = refs[2:2 + npg]
    v_refs = refs[2 + npg:2 + 2 * npg]
    knew_ref, vnew_ref = refs[2 + 2 * npg], refs[3 + 2 * npg]
    pos = 4 + 2 * npg
    if mode == "da":
        lp_ref = refs[pos]
        pos += 1
    else:
        bias_ref, bnew_ref = refs[pos], refs[pos + 1]
        pos += 2
    o_ref, m_scr, l_scr, acc_scr = refs[pos:pos + 4]
    j = pl.program_id(1)
    wq = wq_ref[0]

    @pl.when(j == 0)
    def _():
        s_new = jnp.sum(wq.astype(F32) * knew_ref[0], axis=-1, keepdims=True)
        if mode == "dsa":
            s_new = s_new + bnew_ref[0][:, 0:1].astype(F32)
        m_scr[...] = s_new
        l_scr[...] = jnp.ones(l_scr.shape, F32)
        acc_scr[...] = jnp.broadcast_to(vnew_ref[0], acc_scr.shape)

    def token_rows(ref):
        n_kv = ref.shape[2] // PAGE_SIZE
        return jnp.concatenate([ref[0, 0, pl.ds(kv, PAGE_SIZE, stride=n_kv), :] for kv in range(n_kv)], axis=1)

    v = jnp.concatenate([token_rows(r) for r in v_refs], axis=0).astype(BF16)
    if mode == "da":
        kt = jnp.concatenate([r[0, 0].reshape(-1, PAGE_SIZE) for r in k_refs], axis=1).astype(BF16)
        s = jnp.dot(wq, kt, preferred_element_type=F32)
    else:
        k = jnp.concatenate([token_rows(r) for r in k_refs], axis=0).astype(BF16)
        s = lax.dot_general(wq, k, (((1,), (1,)), ((), ())), preferred_element_type=F32)
        s = s + bias_ref[0].astype(F32)
    _softmax_step(s, v, m_scr, l_scr, acc_scr)

    @pl.when(j == n_pages // npg - 1)
    def _():
        o = acc_scr[...] / l_scr[...]
        if mode == "da":
            lam = _lambda_of(lp_ref[...], lam_init)
            for kv in range(DA_KV_HEADS):
                for g in range(2):
                    r = kv * 4 + g * 2
                    blk = slice(kv * DA_V_DIM, (kv + 1) * DA_V_DIM)
                    h = kv * 2 + g
                    o_ref[0, :, h * DA_V_DIM:(h + 1) * DA_V_DIM] = o[r:r + 1, blk] - lam * o[r + 1:r + 2, blk]
        else:
            for kv in range(DSA_KV_HEADS):
                for g in range(2):
                    r = kv * 2 + g
                    o_ref[0, :, r * DSA_HEAD_DIM:(r + 1) * DSA_HEAD_DIM] = (
                        o[r:r + 1, kv * DSA_HEAD_DIM:(kv + 1) * DSA_HEAD_DIM])


def _paged_attn(mode, layer, page_table, wq, k_cache, v_cache, k_new, v_new, extra, lam_init):
    nb, n_pages = page_table.shape
    npg = math.gcd(PAGES_PER_STEP, n_pages)
    width = k_new.shape[2]

    def page_spec(cache, i):
        tail = cache.shape[2:]
        zeros = (0,) * len(tail)
        return pl.BlockSpec((1, 1) + tail, lambda b, j, pt: (layer, pt[b * n_pages + j * npg + i]) + zeros)

    per_seq = lambda b, j, pt: (b, 0, 0)
    in_specs = [pl.BlockSpec((1, STREAM_ROWS, width), per_seq)]
    in_specs += [page_spec(k_cache, i) for i in range(npg)] + [page_spec(v_cache, i) for i in range(npg)]
    in_specs += [pl.BlockSpec((1, 1, width), per_seq), pl.BlockSpec((1, 1, width), per_seq)]
    args = [wq] + [k_cache] * npg + [v_cache] * npg + [k_new, v_new]
    if mode == "da":
        in_specs.append(pl.BlockSpec((4, DA_QK_DIM), lambda b, j, pt: (0, 0)))
        args.append(extra)
    else:
        in_specs.append(pl.BlockSpec((1, 1, npg * PAGE_SIZE), lambda b, j, pt: (b, 0, j)))
        in_specs.append(pl.BlockSpec((1, 1, PAGE_SIZE), lambda b, j, pt: (b, 0, n_pages)))
        args += [extra, extra]
    grid_spec = pltpu.PrefetchScalarGridSpec(
        num_scalar_prefetch=1,
        grid=(nb, n_pages // npg),
        in_specs=in_specs,
        out_specs=pl.BlockSpec((1, 1, 8 * LANES), per_seq),
        scratch_shapes=[pltpu.VMEM((STREAM_ROWS, 1), F32), pltpu.VMEM((STREAM_ROWS, 1), F32),
                        pltpu.VMEM((STREAM_ROWS, width), F32)],
    )
    return pl.pallas_call(
        functools.partial(_paged_attn_kernel, mode, lam_init, n_pages, npg),
        out_shape=jax.ShapeDtypeStruct((nb, 1, 8 * LANES), F32),
        grid_spec=grid_spec,
        compiler_params=_cparams("parallel", "arbitrary"),
        name="paged_" + mode,
    )(page_table.reshape(-1), *args)


def _paged_index_kernel(*refs):
    npg = len(refs) - 4
    q_ref, w_ref = refs[1], refs[2]
    k_refs = refs[3:3 + npg]
    o_ref = refs[3 + npg]
    kt = jnp.concatenate([r[0, 0] for r in k_refs], axis=1).astype(BF16)
    d = jnp.dot(q_ref[0], kt, preferred_element_type=F32)
    o_ref[0] = jnp.sum(jnp.maximum(d, 0.0) * w_ref[0], axis=0, keepdims=True)


def _paged_index(layer, page_table, qi, wi, k_cache):
    nb, n_pages = page_table.shape
    npg = math.gcd(INDEX_PAGES_PER_STEP, n_pages)

    def page_spec(i):
        return pl.BlockSpec((1, 1, IDX_DIM, PAGE_SIZE),
                            lambda b, j, pt: (layer, pt[b * n_pages + j * npg + i], 0, 0))

    per_seq = lambda b, j, pt: (b, 0, 0)
    grid_spec = pltpu.PrefetchScalarGridSpec(
        num_scalar_prefetch=1,
        grid=(nb, n_pages // npg),
        in_specs=[pl.BlockSpec((1, IDX_HEADS, IDX_DIM), per_seq), pl.BlockSpec((1, IDX_HEADS, 1), per_seq)]
        + [page_spec(i) for i in range(npg)],
        out_specs=pl.BlockSpec((1, 1, npg * PAGE_SIZE), lambda b, j, pt: (b, 0, j)),
    )
    return pl.pallas_call(
        _paged_index_kernel,
        out_shape=jax.ShapeDtypeStruct((nb, 1, n_pages * PAGE_SIZE), F32),
        grid_spec=grid_spec,
        compiler_params=_cparams("parallel", "arbitrary"),
        name="paged_index",
    )(page_table.reshape(-1), qi, wi, *([k_cache] * npg))


def _sample_select_kernel(topk, sc_ref, qi_ref, kin_ref, wi_ref, o_ref, key_scr):
    nb, past = sc_ref.shape
    width = past + LANES
    prod = qi_ref[...] * kin_ref[...]
    lane = lax.broadcasted_iota(I32, (nb, LANES), 1)
    wi = wi_ref[...]
    s_new = jnp.zeros((nb, 1), F32)
    for h in range(IDX_HEADS):
        blk = prod[:, (h // 2) * LANES:(h // 2 + 1) * LANES]
        keep = (lane < IDX_DIM) if h % 2 == 0 else (lane >= IDX_DIM)
        d = jnp.sum(jnp.where(keep, blk, 0.0), axis=-1, keepdims=True)
        s_new = s_new + jnp.maximum(d, 0.0) * wi[:, h:h + 1]
    key_scr[:, 0:past] = _float_key(sc_ref[...])
    key_scr[:, past:width] = jnp.where(lane == 0, _float_key(s_new), INT_MIN)
    col = lax.broadcasted_iota(I32, (nb, width), 1)

    def count(pred):
        return jnp.sum(jnp.where(pred(key_scr[...], col), 1.0, 0.0), axis=-1, keepdims=True)

    kf = float(topk)
    t = _kth_largest_key(lambda cand: count(lambda keys, idx: keys >= cand), nb, kf)
    t = jnp.maximum(t, INT_MIN + 1)
    n_ge = count(lambda keys, idx: keys >= t)

    @pl.when(jnp.max(n_ge) > kf)
    def _():
        need = kf - count(lambda keys, idx: keys > t)

        def bit_body(i, x):
            cand = x + lax.shift_left(jnp.int32(1), 14 - i)
            below = count(lambda keys, idx: jnp.logical_and(keys == t, idx < cand))
            return jnp.where(below < need, cand, x)

        last = lax.fori_loop(0, 15, bit_body, jnp.zeros((nb, 1), I32))
        keys = key_scr[...]
        key_scr[...] = jnp.where(jnp.logical_and(keys == t, col > last), INT_MIN, keys)

    o_ref[...] = jnp.where(key_scr[...] >= t, 0.0, NEG).astype(BF16)


def _sample_select(scores, qi, ki_new8, wi, topk):
    nb, past = scores.shape
    assert past + LANES < 2 ** 15
    return pl.pallas_call(
        functools.partial(_sample_select_kernel, topk),
        out_shape=jax.ShapeDtypeStruct((nb, past + LANES), BF16),
        scratch_shapes=[pltpu.VMEM((nb, past + LANES), I32)],
        compiler_params=pltpu.CompilerParams(vmem_limit_bytes=VMEM_LIMIT_BYTES),
        name="sample_select",
    )(scores, qi, ki_new8, wi)


def _rope_tables(pos, d):
    r = d // 4
    half = r // 2
    inv = 1.0 / (ROPE_THETA ** (jnp.arange(half, dtype=F32) * (2.0 / r)))
    ang = pos.astype(F32)[:, None] * inv[None, :]
    cos, sin = jnp.cos(ang), jnp.sin(ang)
    n = pos.shape[0]
    c = jnp.concatenate([cos, cos, jnp.ones((n, d - r), F32)], axis=1)
    s = jnp.concatenate([-sin, sin, jnp.zeros((n, d - r), F32)], axis=1)
    reps = LANES // d
    return jnp.tile(c, (1, reps)), jnp.tile(s, (1, reps))


def _pack_w_in(w):
    parts = jnp.split(w, np.cumsum(SPLITS).tolist(), axis=1)
    qa, ka, va, qd, kd, vd, qi, ki, wi, gates = parts
    ga, gb = jnp.split(gates, 2, axis=1)
    pad = jnp.zeros((w.shape[0], PROJ_TN - IDX_DIM - IDX_HEADS), w.dtype)
    return jnp.concatenate([qa, ka, va, qd, kd, vd, ga, gb, qi, ki, wi, pad], axis=1).astype(BF16)


def _cols(z, name):
    off, width = _OFF[name]
    return z[:, off:off + width]


def _key_tiles(k, n_heads):
    t = k.shape[0]
    kt = k.reshape(t // TK, TK, n_heads, LANES)
    return jnp.transpose(kt, (2, 0, 3, 1))


def _with_ones(v, n_heads):
    t = v.shape[0]
    v3 = v.reshape(t, n_heads, LANES)
    return jnp.concatenate([v3, jnp.ones_like(v3)], axis=2).reshape(t, n_heads * 2 * LANES)


def _layer_weights(l, p):
    return dict(
        norm_ffn1=p["norm_ffn1"][l], w_gu1=p["ffn1_w_gu"][l].astype(BF16), w_d1=p["ffn1_w_down"][l].astype(BF16),
        norm_mix=p["norm_mix"][l], w_in=_pack_w_in(p["w_in"][l]),
        gains=tuple(jnp.tile(p[n][l], LANES // p[n].shape[1]).reshape(1, LANES)
                    for n in ("da_q_norm", "da_k_norm", "dsa_q_norm", "dsa_k_norm")),
        lp=p["da_lambda"][l], out_norm=p["da_out_norm"][l],
        wa=p["w_branch_a"][l].astype(BF16), wb=p["w_branch_b"][l].astype(BF16), wo=p["w_out"][l].astype(BF16),
        norm_ffn2=p["norm_ffn2"][l], w_gu2=p["ffn2_w_gu"][l].astype(BF16), w_d2=p["ffn2_w_down"][l].astype(BF16),
    )


def _prompt_layer(x, w, tabs, lam_init, topk, t_real):
    x = _ffn(x, w["norm_ffn1"], w["w_gu1"], w["w_d1"], TM_FFN)
    z, zb = _proj(x, w["norm_mix"], w["w_in"], tabs, w["gains"], TM_PROJ,
                  DA_QK_DIM ** -0.5 * LOG2E, DSA_HEAD_DIM ** -0.5 * LOG2E)
    ka, va, kd, vd = _cols(z, "ka"), _cols(z, "va"), _cols(z, "kd"), _cols(z, "vd")
    kw = _cols(z, "kw")
    ki = kw[:, :IDX_DIM]
    oa = _diff_attn(_cols(zb, "qa"), _key_tiles(_cols(zb, "ka"), DA_KV_HEADS),
                    _with_ones(_cols(zb, "va"), DA_KV_HEADS), w["lp"], lam_init, t_real)
    kib = _cols(zb, "kw")[:, :IDX_DIM]
    ki2 = _key_tiles(jnp.concatenate([kib, kib], axis=1), 1)[0]
    wi = kw[:, IDX_DIM:IDX_DIM + LANES]
    bias = _select(_cols(zb, "qi"), wi, ki2, topk, t_real)
    od = _dsa_attn(_cols(zb, "qd"), _key_tiles(_cols(zb, "kd"), DSA_KV_HEADS),
                   _with_ones(_cols(zb, "vd"), DSA_KV_HEADS), bias, t_real)
    x = _merge(x, oa, od, z, w["out_norm"], w["wa"], w["wb"], w["wo"], lam_init, TM_MERGE)
    x = _ffn(x, w["norm_ffn2"], w["w_gu2"], w["w_d2"], TM_FFN)
    return x, (ka, va, kd, vd, ki)


def _expand_queries(q, n_kv, per_kv_rows, width, scale):
    nb = q.shape[0]
    eye = jnp.eye(n_kv, dtype=q.dtype)
    out = jnp.einsum("bkrw,kc->bkrcw", q * scale, eye).reshape(nb, n_kv * per_kv_rows, n_kv * width)
    pad = STREAM_ROWS - n_kv * per_kv_rows
    return jnp.pad(out, ((0, 0), (0, pad), (0, 0))).astype(BF16)


def _cache_views(cache_da_k, cache_da_v, cache_dsa_k, cache_dsa_v, cache_idx_k):
    depth, n_pool = cache_da_k.shape[:2]
    rows = lambda c: c.reshape(depth, n_pool, -1, c.shape[-1])
    return (jnp.transpose(cache_da_k, (0, 1, 3, 4, 5, 2)), rows(cache_da_v), rows(cache_dsa_k),
            rows(cache_dsa_v), jnp.transpose(cache_idx_k, (0, 1, 3, 2)))


def _sample_layer(x, w, tabs, lam_init, topk, layer, caches, page_table):
    da_kt, da_v, dsa_k, dsa_v, idx_kt = caches
    nb = page_table.shape[0]
    rows = x.shape[0]
    x = _ffn(x, w["norm_ffn1"], w["w_gu1"], w["w_d1"], rows)
    z, _ = _proj(x, w["norm_mix"], w["w_in"], tabs, w["gains"], rows, None, None)
    ka, va, kd, vd = (_cols(z, n)[:nb] for n in ("ka", "va", "kd", "vd"))
    kw = _cols(z, "kw")[:nb]
    ki = kw[:, :IDX_DIM]
    wi = kw[:, IDX_DIM:IDX_DIM + IDX_HEADS]
    qa = _cols(z, "qa")[:nb].reshape(nb, DA_KV_HEADS, 2, 2, DA_QK_DIM)
    blk = jnp.einsum("bkgmd,mn->bkgmnd", qa * DA_QK_DIM ** -0.5, jnp.eye(2, dtype=F32))
    wq_da = _expand_queries(blk.reshape(nb, DA_KV_HEADS, 4, 2 * DA_QK_DIM), DA_KV_HEADS, 4, LANES, 1.0)
    oa = _paged_attn("da", layer, page_table, wq_da, da_kt, da_v,
                     ka.reshape(nb, 1, -1), va.reshape(nb, 1, -1), w["lp"], lam_init)
    qi = _cols(z, "qi")[:nb]
    scores = _paged_index(layer, page_table, qi.reshape(nb, IDX_HEADS, IDX_DIM).astype(BF16),
                          wi.reshape(nb, IDX_HEADS, 1), idx_kt)
    bias = _sample_select(scores.reshape(nb, -1), qi, jnp.tile(ki, (1, IDX_HEADS)),
                          jnp.pad(wi, ((0, 0), (0, LANES - IDX_HEADS))), topk)
    qd = _cols(z, "qd")[:nb].reshape(nb, DSA_KV_HEADS, 2, DSA_HEAD_DIM)
    wq_dsa = _expand_queries(qd, DSA_KV_HEADS, 2, DSA_HEAD_DIM, DSA_HEAD_DIM ** -0.5)
    od = _paged_attn("dsa", layer, page_table, wq_dsa, dsa_k, dsa_v,
                     kd.reshape(nb, 1, -1), vd.reshape(nb, 1, -1), bias.reshape(nb, 1, -1), None)
    pad = ((0, rows - nb), (0, 0))
    x = _merge(x, jnp.pad(oa.reshape(nb, -1), pad), jnp.pad(od.reshape(nb, -1), pad), z,
               w["out_norm"], w["wa"], w["wb"], w["wo"], lam_init, rows)
    x = _ffn(x, w["norm_ffn2"], w["w_gu2"], w["w_d2"], rows)
    return x, (ka, va, kd, vd, ki)


def kernel(x_prompt, x_sample, cache_da_k, cache_da_v, cache_dsa_k, cache_dsa_v, cache_idx_k, page_table,
           meta_tokens, norm_ffn1, ffn1_w_gu, ffn1_w_down, norm_mix, w_in, da_q_norm, da_k_norm,
           dsa_q_norm, dsa_k_norm, da_lambda, da_out_norm, w_branch_a, w_branch_b, w_out,
           norm_ffn2, ffn2_w_gu, ffn2_w_down):
    params = dict(norm_ffn1=norm_ffn1, ffn1_w_gu=ffn1_w_gu, ffn1_w_down=ffn1_w_down, norm_mix=norm_mix,
                  w_in=w_in, da_q_norm=da_q_norm, da_k_norm=da_k_norm, dsa_q_norm=dsa_q_norm,
                  dsa_k_norm=dsa_k_norm, da_lambda=da_lambda, da_out_norm=da_out_norm,
                  w_branch_a=w_branch_a, w_branch_b=w_branch_b, w_out=w_out, norm_ffn2=norm_ffn2,
                  ffn2_w_gu=ffn2_w_gu, ffn2_w_down=ffn2_w_down)
    depth, d_model = norm_mix.shape
    assert w_in.shape[2] == sum(SPLITS) + 2 * d_model and d_model == DA_HEADS * DA_V_DIM
    assert x_prompt.shape[0] == 1 and x_sample.shape[1] == 1
    weights = [_layer_weights(l, params) for l in range(depth)]
    lam_inits = [0.8 - 0.6 * math.exp(-0.3 * l) for l in range(depth)]

    seq = x_prompt.shape[1]
    t = seq + N_META
    t_pad = -(-t // T_ALIGN) * T_ALIGN
    xp = jnp.concatenate([meta_tokens.astype(F32), x_prompt[0], jnp.zeros((t_pad - t, d_model), F32)], axis=0)
    pos_p = jnp.arange(t_pad, dtype=jnp.int32)
    tabs_p = _rope_tables(pos_p, DA_QK_DIM) + _rope_tables(pos_p, DSA_HEAD_DIM)
    topk_p = min(TOPK_MAX, t // 4)
    rows_p = []
    for l in range(depth):
        xp, rows = _prompt_layer(xp, weights[l], tabs_p, lam_inits[l], topk_p, t)
        rows_p.append(tuple(r[:t] for r in rows))
    y_prompt = xp[N_META:t][None]

    nb, n_pages = page_table.shape
    past = n_pages * PAGE_SIZE
    rows_s_pad = -(-nb // 16) * 16
    xs = jnp.pad(x_sample[:, 0, :], ((0, rows_s_pad - nb), (0, 0)))
    pos_s = jnp.full((rows_s_pad,), past, jnp.int32)
    tabs_s = _rope_tables(pos_s, DA_QK_DIM) + _rope_tables(pos_s, DSA_HEAD_DIM)
    topk_s = min(TOPK_MAX, (past + 1) // 4)
    rows_s = []
    caches = _cache_views(cache_da_k, cache_da_v, cache_dsa_k, cache_dsa_v, cache_idx_k)
    for l in range(depth):
        xs, rows = _sample_layer(xs, weights[l], tabs_s, lam_inits[l], topk_s, l, caches, page_table)
        rows_s.append(rows)
    y_sample = xs[:nb][:, None, :]

    def stack(rows, i, shape):
        return jnp.stack([r[i] for r in rows], axis=0).reshape((depth,) + shape)

    lead_p, lead_s = (1, t), (nb, 1)
    return (y_prompt, y_sample,
            stack(rows_p, 0, lead_p + (DA_KV_HEADS, 2, DA_QK_DIM)), stack(rows_p, 1, lead_p + (DA_KV_HEADS, DA_V_DIM)),
            stack(rows_p, 2, lead_p + (DSA_KV_HEADS, DSA_HEAD_DIM)), stack(rows_p, 3, lead_p + (DSA_KV_HEADS, DSA_HEAD_DIM)),
            stack(rows_p, 4, lead_p + (IDX_DIM,)),
            stack(rows_s, 0, lead_s + (DA_KV_HEADS, 2, DA_QK_DIM)), stack(rows_s, 1, lead_s + (DA_KV_HEADS, DA_V_DIM)),
            stack(rows_s, 2, lead_s + (DSA_KV_HEADS, DSA_HEAD_DIM)), stack(rows_s, 3, lead_s + (DSA_KV_HEADS, DSA_HEAD_DIM)),
            stack(rows_s, 4, lead_s + (IDX_DIM,)))
```
